```python
import jax, jax.numpy as jnp
from jax import lax
import numpy as np

D_MODEL = 1024
BATCH = 8
SEQ = 4096
DEPTH = 2
DEC_BATCH = 32
DEC_SEQ = 32
PAST_LEN = 2048

CHUNK = 64
Q_BLOCK = 128
MIX_WIDTH = D_MODEL
FOX_HEADS = 8
FOX_HEAD_DIM = MIX_WIDTH // 2 // FOX_HEADS
FOX_WIDTH = FOX_HEADS * FOX_HEAD_DIM
MLA_HEADS = 4
MLA_NOPE_DIM = 128
MLA_ROPE_DIM = 64
MLA_V_DIM = MIX_WIDTH // 2 // MLA_HEADS
MLA_WIDTH = MLA_HEADS * MLA_V_DIM
MLA_Q_RANK = 256
MLA_KV_RANK = 128
ROPE_THETA = 10000.0
N_MEM = 256
X_HEADS = 4
X_HEAD_DIM = D_MODEL // X_HEADS
D_FF = 4 * D_MODEL
EPS = 1e-6
FORGET_BIAS = 2.0
IN_SPLITS = [FOX_WIDTH, 2 * FOX_WIDTH, 3 * FOX_WIDTH, 3 * FOX_WIDTH + FOX_HEADS,
             3 * FOX_WIDTH + FOX_HEADS + MLA_Q_RANK,
             3 * FOX_WIDTH + FOX_HEADS + MLA_Q_RANK + MLA_KV_RANK]
IN_COLS = 3 * FOX_WIDTH + FOX_HEADS + MLA_Q_RANK + MLA_KV_RANK + MLA_ROPE_DIM

kernel_name = "hybrid_fox_mla_streaming_encoder_step"


def rmsnorm(x, g):
    xf = x.astype(jnp.float32)
    y = xf * lax.rsqrt(jnp.mean(jnp.square(xf), axis=-1, keepdims=True) + EPS)
    return (y * g.astype(jnp.float32)).astype(x.dtype)


def rope(x, pos):
    half = x.shape[-1] // 2
    inv = ROPE_THETA ** (-jnp.arange(half, dtype=jnp.float32) / half)
    ang = pos.astype(jnp.float32)[:, None] * inv[None, :]
    shape = (1, x.shape[1]) + (1,) * (x.ndim - 3) + (half,)
    cos = jnp.cos(ang).reshape(shape)
    sin = jnp.sin(ang).reshape(shape)
    xf = x.astype(jnp.float32)
    x1, x2 = xf[..., :half], xf[..., half:]
    return jnp.concatenate([x1 * cos - x2 * sin, x2 * cos + x1 * sin], axis=-1).astype(x.dtype)


def attend(q, k, v, q_pos, k_pos, mask_kind, f_q=None, f_k=None):
    scale = q.shape[-1] ** -0.5
    s = jnp.einsum("bqhd,bkhd->bhqk", q, k).astype(jnp.float32) * scale
    if f_q is not None:
        s = s + (jnp.swapaxes(f_q, 1, 2)[:, :, :, None] - jnp.swapaxes(f_k, 1, 2)[:, :, None, :])
    if mask_kind == "frame":
        allowed = k_pos[None, :] <= q_pos[:, None]
    else:
        allowed = (k_pos[None, :] // CHUNK) <= (q_pos[:, None] // CHUNK)
    s = jnp.where(allowed[None, None], s, -jnp.inf)
    p = jax.nn.softmax(s, axis=-1)
    return jnp.einsum("bhqk,bkhd->bqhd", p.astype(v.dtype), v)


def sweep_attention(q, k, v, q_pos, k_pos, mask_kind, f_q=None, f_k=None):
    b, sq, h, _ = q.shape
    if sq % Q_BLOCK != 0:
        return attend(q, k, v, q_pos, k_pos, mask_kind, f_q, f_k)
    nb = sq // Q_BLOCK

    def to_blocks(a):
        return jnp.moveaxis(a.reshape((b, nb, Q_BLOCK) + a.shape[2:]), 1, 0)

    qb = to_blocks(q)
    pb = q_pos.reshape(nb, Q_BLOCK)
    if f_q is None:
        out = lax.map(lambda a: attend(a[0], k, v, a[1], k_pos, mask_kind), (qb, pb))
    else:
        out = lax.map(lambda a: attend(a[0], k, v, a[1], k_pos, mask_kind, a[2], f_k),
                      (qb, pb, to_blocks(f_q)))
    return jnp.moveaxis(out, 0, 1).reshape(b, sq, h, v.shape[-1])


def memory_kv(mem, g, w_mk, w_mv):
    b, n, _ = mem.shape
    m = rmsnorm(mem, g)
    return ((m @ w_mk).reshape(b, n, X_HEADS, X_HEAD_DIM),
            (m @ w_mv).reshape(b, n, X_HEADS, X_HEAD_DIM))


def cross_attention(h, mk, mv, w_xq, w_xo):
    b, s, _ = h.shape
    q = (h @ w_xq).reshape(b, s, X_HEADS, X_HEAD_DIM)
    sc = jnp.einsum("bqhd,bkhd->bhqk", q, mk).astype(jnp.float32) * (X_HEAD_DIM ** -0.5)
    p = jax.nn.softmax(sc, axis=-1)
    o = jnp.einsum("bhqk,bkhd->bqhd", p.astype(mv.dtype), mv)
    return o.reshape(b, s, X_HEADS * X_HEAD_DIM) @ w_xo


def setup_inputs(seed: int = 0) -> dict:
    key = jax.random.key(seed)
    ks = jax.random.split(key, 28)
    f32 = jnp.float32

    def nrm(i, shape, scale=1.0):
        return jax.random.normal(ks[i], shape, f32) * scale

    def gain(i, shape):
        return 1.0 + 0.1 * nrm(i, shape)

    return {
        "x_prompt": nrm(0, (BATCH, SEQ, D_MODEL)),
        "x_sample": nrm(1, (DEC_BATCH, DEC_SEQ, D_MODEL)),
        "mem_prompt": nrm(2, (BATCH, N_MEM, D_MODEL)),
        "cache_fox_k": nrm(3, (DEPTH, DEC_BATCH, PAST_LEN, FOX_HEADS, FOX_HEAD_DIM)),
        "cache_fox_v": nrm(4, (DEPTH, DEC_BATCH, PAST_LEN, FOX_HEADS, FOX_HEAD_DIM)),
        "cache_fox_logf": jax.nn.log_sigmoid(FORGET_BIAS + nrm(5, (DEPTH, DEC_BATCH, PAST_LEN, FOX_HEADS))),
        "cache_mla_ckv": nrm(6, (DEPTH, DEC_BATCH, PAST_LEN, MLA_KV_RANK)),
        "cache_mla_krope": nrm(7, (DEPTH, DEC_BATCH, PAST_LEN, MLA_ROPE_DIM)),
        "cache_mem_k": nrm(8, (DEPTH, DEC_BATCH, N_MEM, X_HEADS, X_HEAD_DIM)),
        "cache_mem_v": nrm(9, (DEPTH, DEC_BATCH, N_MEM, X_HEADS, X_HEAD_DIM)),
        "norm_mix": gain(10, (DEPTH, D_MODEL)),
        "w_in": nrm(11, (DEPTH, D_MODEL, IN_COLS), D_MODEL ** -0.5),
        "b_forget": FORGET_BIAS + 0.1 * nrm(12, (DEPTH, FOX_HEADS)),
        "mla_q_norm": gain(13, (DEPTH, MLA_Q_RANK)),
        "w_uq": nrm(14, (DEPTH, MLA_Q_RANK, MLA_HEADS * (MLA_NOPE_DIM + MLA_ROPE_DIM)), MLA_Q_RANK ** -0.5),
        "mla_kv_norm": gain(15, (DEPTH, MLA_KV_RANK)),
        "w_ukv": nrm(16, (DEPTH, MLA_KV_RANK, MLA_HEADS * (MLA_NOPE_DIM + MLA_V_DIM)), MLA_KV_RANK ** -0.5),
        "w_out": nrm(17, (DEPTH, MIX_WIDTH, D_MODEL), MIX_WIDTH ** -0.5),
        "norm_cross": gain(18, (DEPTH, D_MODEL)),
        "norm_mem": gain(19, (DEPTH, D_MODEL)),
        "w_xq": nrm(20, (DEPTH, D_MODEL, X_HEADS * X_HEAD_DIM), D_MODEL ** -0.5),
        "w_mk": nrm(21, (DEPTH, D_MODEL, X_HEADS * X_HEAD_DIM), D_MODEL ** -0.5),
        "w_mv": nrm(22, (DEPTH, D_MODEL, X_HEADS * X_HEAD_DIM), D_MODEL ** -0.5),
        "w_xo": nrm(23, (DEPTH, X_HEADS * X_HEAD_DIM, D_MODEL), (X_HEADS * X_HEAD_DIM) ** -0.5),
        "norm_mlp": gain(24, (DEPTH, D_MODEL)),
        "w_up": nrm(25, (DEPTH, D_MODEL, D_FF), D_MODEL ** -0.5),
        "w_down": nrm(26, (DEPTH, D_FF, D_MODEL), D_FF ** -0.5),
        "norm_final": gain(27, (D_MODEL,)),
    }


def reference(x_prompt, x_sample, mem_prompt, cache_fox_k, cache_fox_v, cache_fox_logf,
              cache_mla_ckv, cache_mla_krope, cache_mem_k, cache_mem_v,
              norm_mix, w_in, b_forget, mla_q_norm, w_uq, mla_kv_norm, w_ukv, w_out,
              norm_cross, norm_mem, w_xq, w_mk, w_mv, w_xo, norm_mlp, w_up, w_down, norm_final):

    def mixers(h, pos, l, past):
        b, s, _ = h.shape
        z = h @ w_in[l]
        q_f, k_f, v_f, g_f, c_q, c_kv, k_r = jnp.split(z, IN_SPLITS, axis=-1)
        q_f = q_f.reshape(b, s, FOX_HEADS, FOX_HEAD_DIM)
        k_f = k_f.reshape(b, s, FOX_HEADS, FOX_HEAD_DIM)
        v_f = v_f.reshape(b, s, FOX_HEADS, FOX_HEAD_DIM)
        logf = jax.nn.log_sigmoid((g_f + b_forget[l]).astype(jnp.float32))
        c_kv = rmsnorm(c_kv, mla_kv_norm[l])
        k_r = rope(k_r, pos)
        rows = (k_f, v_f, logf, c_kv, k_r)
        if past is None:
            kf, vf, lf, ckv, kr, k_pos = k_f, v_f, logf, c_kv, k_r, pos
        else:
            pk, pv, plf, pckv, pkr = past
            kf = jnp.concatenate([pk, k_f], axis=1)
            vf = jnp.concatenate([pv, v_f], axis=1)
            lf = jnp.concatenate([plf.astype(jnp.float32), logf], axis=1)
            ckv = jnp.concatenate([pckv, c_kv], axis=1)
            kr = jnp.concatenate([pkr, k_r], axis=1)
            k_pos = jnp.arange(pk.shape[1] + s)
        cum = jnp.cumsum(lf, axis=1)
        fox = sweep_attention(q_f, kf, vf, pos, k_pos, "frame", cum[:, -s:], cum)
        sk = ckv.shape[1]
        q_m = (rmsnorm(c_q, mla_q_norm[l]) @ w_uq[l]).reshape(b, s, MLA_HEADS, MLA_NOPE_DIM + MLA_ROPE_DIM)
        q_m = jnp.concatenate([q_m[..., :MLA_NOPE_DIM], rope(q_m[..., MLA_NOPE_DIM:], pos)], axis=-1)
        kv = (ckv @ w_ukv[l]).reshape(b, sk, MLA_HEADS, MLA_NOPE_DIM + MLA_V_DIM)
        k_m = jnp.concatenate([kv[..., :MLA_NOPE_DIM],
                               jnp.broadcast_to(kr[:, :, None, :], (b, sk, MLA_HEADS, MLA_ROPE_DIM))], axis=-1)
        mla = sweep_attention(q_m, k_m, kv[..., MLA_NOPE_DIM:], pos, k_pos, "chunk")
        mixed = jnp.concatenate([fox.reshape(b, s, FOX_WIDTH), mla.reshape(b, s, MLA_WIDTH)], axis=-1)
        return mixed @ w_out[l], rows

    def layer(x, pos, l, past, mk, mv):
        mix, rows = mixers(rmsnorm(x, norm_mix[l]), pos, l, past)
        x = x + mix
        x = x + cross_attention(rmsnorm(x, norm_cross[l]), mk, mv, w_xq[l], w_xo[l])
        hm = rmsnorm(x, norm_mlp[l])
        x = x + jnp.square(jax.nn.relu(hm @ w_up[l])) @ w_down[l]
        return x, rows

    pos_p = jnp.arange(x_prompt.shape[1])
    pos_s = cache_fox_k.shape[2] + jnp.arange(x_sample.shape[1])
    xp, xs = x_prompt, x_sample
    rows_p, rows_s, mk_p, mv_p = [], [], [], []
    for l in range(DEPTH):
        mk, mv = memory_kv(mem_prompt, norm_mem[l], w_mk[l], w_mv[l])
        xp, rp = layer(xp, pos_p, l, None, mk, mv)
        past = (cache_fox_k[l], cache_fox_v[l], cache_fox_logf[l], cache_mla_ckv[l], cache_mla_krope[l])
        xs, rs = layer(xs, pos_s, l, past, cache_mem_k[l], cache_mem_v[l])
        rows_p.append(rp)
        rows_s.append(rs)
        mk_p.append(mk)
        mv_p.append(mv)

    def stack(rows, i):
        return jnp.stack([r[i] for r in rows])

    y_prompt = rmsnorm(xp, norm_final)
    y_sample = rmsnorm(xs, norm_final)
    return (y_prompt, y_sample,
            stack(rows_p, 0), stack(rows_p, 1), stack(rows_p, 2), stack(rows_p, 3), stack(rows_p, 4),
            jnp.stack(mk_p), jnp.stack(mv_p),
            stack(rows_s, 0), stack(rows_s, 1), stack(rows_s, 2), stack(rows_s, 3), stack(rows_s, 4))
```

```python
import functools

import jax
import jax.numpy as jnp
from jax import lax
from jax.experimental import pallas as pl
from jax.experimental.pallas import tpu as pltpu

F32 = jnp.float32
BF16 = jnp.bfloat16

D_MODEL = 1024
CHUNK = 64
FOX_HEADS = 8
FOX_HEAD_DIM = 64
FOX_WIDTH = FOX_HEADS * FOX_HEAD_DIM
MLA_HEADS = 4
MLA_NOPE_DIM = 128
MLA_ROPE_DIM = 64
MLA_V_DIM = 128
MLA_WIDTH = MLA_HEADS * MLA_V_DIM
MLA_Q_RANK = 256
MLA_KV_RANK = 128
ROPE_THETA = 10000.0
X_HEADS = 4
X_HEAD_DIM = 256
D_FF = 4 * D_MODEL
EPS = 1e-6

LANES = 128
PAIR = 2 * LANES
BIAS_SLOTS = 8
VMEM_LIMIT = 56 * 1024 * 1024

_OQ, _OK, _OV = 0, FOX_WIDTH, 2 * FOX_WIDTH
_OCQ = 3 * FOX_WIDTH
_OCKV = _OCQ + MLA_Q_RANK
_OKR = _OCKV + MLA_KV_RANK
_OG = _OKR + LANES
IN_COLS_P = _OG + LANES


def _cparams(sem):
    return pltpu.CompilerParams(dimension_semantics=sem, vmem_limit_bytes=VMEM_LIMIT)


def _rms(x, g):
    return x * lax.rsqrt(jnp.mean(jnp.square(x), axis=-1, keepdims=True) + EPS) * g


def _dot(a, b):
    return jnp.dot(a, b, preferred_element_type=F32)


def _dot_nt(a, b):
    return lax.dot_general(a, b, (((1,), (1,)), ((), ())), preferred_element_type=F32)


def _split3(x):
    x1 = x.astype(BF16)
    r1 = x - x1.astype(F32)
    x2 = r1.astype(BF16)
    r2 = r1 - x2.astype(F32)
    return x1, x2, r2.astype(BF16)


def _cumsum_rows(x, carry):
    n = x.shape[0]
    blk = min(n, 256)
    r = lax.broadcasted_iota(jnp.int32, (blk, blk), 0)
    c = lax.broadcasted_iota(jnp.int32, (blk, blk), 1)
    tri = (c <= r).astype(BF16)
    outs = []
    for i in range(n // blk):
        x1, x2, x3 = _split3(x[i * blk:(i + 1) * blk])
        cb = _dot(tri, x1) + _dot(tri, x2) + _dot(tri, x3) + carry
        carry = cb[blk - 1:blk]
        outs.append(cb)
    return outs[0] if len(outs) == 1 else jnp.concatenate(outs, axis=0)


def _bias_blocks(cum):
    c1, c2, c3 = (c.astype(F32) for c in _split3(cum))
    lane = lax.broadcasted_iota(jnp.int32, (1, LANES), 1)
    j = lane % BIAS_SLOTS
    valid = lane < FOX_HEADS * BIAS_SLOTS
    one = jnp.where(valid & (j < 6), 1.0, 0.0)
    qb = jnp.where(j == 0, c1, jnp.where(j == 1, c2, jnp.where(j == 2, c3, one)))
    kb = jnp.where(j < 3, one, jnp.where(j == 3, -c1, jnp.where(j == 4, -c2, jnp.where(j == 5, -c3, 0.0))))
    return jnp.where(valid, qb, 0.0).astype(BF16), jnp.where(valid, kb, 0.0).astype(BF16)


def _rope_pair(t, cs):
    u = t * cs
    return u + pltpu.roll(u, MLA_ROPE_DIM, 1)


def _inproj_kernel(x_ref, cin_ref, cs_ref, gmix_ref, win_ref, bblk_ref, gq_ref, wuq_ref, gkv_ref, wukv_ref,
                   kf_ref, vf_ref, lf_ref, ckv_ref, kr_ref,
                   qf2_ref, kf2_ref, vf2_ref, qm2_ref, km2_ref, vm2_ref, carry_ref):
    @pl.when(pl.program_id(1) == 0)
    def _():
        carry_ref[...] = cin_ref[0]

    x = x_ref[0]
    tm = x.shape[0]
    h = _rms(x, gmix_ref[...])
    z = _dot(h.astype(BF16), win_ref[...])
    cs = cs_ref[...]
    lane = lax.broadcasted_iota(jnp.int32, (1, LANES), 1)
    low = lane < MLA_ROPE_DIM
    ones_blk = jnp.ones((tm, LANES), BF16)

    kf_ref[0] = z[:, _OK:_OK + FOX_WIDTH]
    vf_ref[0] = z[:, _OV:_OV + FOX_WIDTH]

    zg = z[:, _OG:_OG + LANES] + bblk_ref[...]
    lf = jnp.minimum(zg, 0.0) - jnp.log1p(jnp.exp(-jnp.abs(zg)))
    lf_ref[0] = lf
    cum = _cumsum_rows(lf, carry_ref[...])
    carry_ref[...] = cum[tm - 1:tm]
    qb, kb = _bias_blocks(cum)

    fox_scale = FOX_HEAD_DIM ** -0.5
    for p in range(FOX_HEADS // 2):
        lo, hi = p * PAIR, p * PAIR + LANES
        qf2_ref[0, :, lo:hi] = (z[:, _OQ + p * LANES:_OQ + (p + 1) * LANES] * fox_scale).astype(BF16)
        qf2_ref[0, :, hi:hi + LANES] = qb
        kf2_ref[0, :, lo:hi] = z[:, _OK + p * LANES:_OK + (p + 1) * LANES].astype(BF16)
        kf2_ref[0, :, hi:hi + LANES] = kb
        vf2_ref[0, :, lo:hi] = z[:, _OV + p * LANES:_OV + (p + 1) * LANES].astype(BF16)
        vf2_ref[0, :, hi:hi + LANES] = ones_blk

    kr = _rope_pair(z[:, _OKR:_OKR + LANES], cs)
    kr_ref[0] = kr[:, :MLA_ROPE_DIM]
    krb = jnp.where(low, kr, 0.0).astype(BF16)

    ckv = _rms(z[:, _OCKV:_OCKV + MLA_KV_RANK], gkv_ref[...])
    ckv_ref[0] = ckv
    kv = _dot(ckv.astype(BF16), wukv_ref[...])
    cqn = _rms(z[:, _OCQ:_OCQ + MLA_Q_RANK], gq_ref[...])
    qm = _dot(cqn.astype(BF16), wuq_ref[...])
    mla_scale = (MLA_NOPE_DIM + MLA_ROPE_DIM) ** -0.5
    for hd in range(MLA_HEADS):
        lo, hi = hd * PAIR, hd * PAIR + LANES
        qr = _rope_pair(qm[:, MLA_HEADS * LANES + hd * LANES:MLA_HEADS * LANES + (hd + 1) * LANES], cs)
        qm2_ref[0, :, lo:hi] = (qm[:, hd * LANES:(hd + 1) * LANES] * mla_scale).astype(BF16)
        qm2_ref[0, :, hi:hi + LANES] = (jnp.where(low, qr, 0.0) * mla_scale).astype(BF16)
        km2_ref[0, :, lo:hi] = kv[:, hd * LANES:(hd + 1) * LANES].astype(BF16)
        km2_ref[0, :, hi:hi + LANES] = krb
        vm2_ref[0, :, lo:hi] = kv[:, MLA_HEADS * LANES + hd * LANES:MLA_HEADS * LANES + (hd + 1) * LANES].astype(BF16)
        vm2_ref[0, :, hi:hi + LANES] = ones_blk


def _inproj(x, cin, cs, gmix, win, bblk, gq, wuq, gkv, wukv, *, tm):
    b, s, _ = x.shape
    grid = (b, s // tm)
    tok = lambda w: pl.BlockSpec((1, tm, w), lambda i, j: (i, j, 0))
    full = lambda a: pl.BlockSpec(a.shape, lambda i, j: (0,) * a.ndim)
    out_shape = (
        jax.ShapeDtypeStruct((b, s, FOX_WIDTH), F32), jax.ShapeDtypeStruct((b, s, FOX_WIDTH), F32),
        jax.ShapeDtypeStruct((b, s, LANES), F32), jax.ShapeDtypeStruct((b, s, MLA_KV_RANK), F32),
        jax.ShapeDtypeStruct((b, s, MLA_ROPE_DIM), F32),
    ) + tuple(jax.ShapeDtypeStruct((b, s, 4 * PAIR), BF16) for _ in range(6))
    out_specs = (tok(FOX_WIDTH), tok(FOX_WIDTH), tok(LANES), tok(MLA_KV_RANK), tok(MLA_ROPE_DIM)) + (tok(4 * PAIR),) * 6
    return pl.pallas_call(
        _inproj_kernel,
        grid=grid,
        in_specs=[tok(D_MODEL), pl.BlockSpec((1, 1, LANES), lambda i, j: (i, 0, 0)),
                  pl.BlockSpec((tm, LANES), lambda i, j: (j, 0)),
                  full(gmix), full(win), full(bblk), full(gq), full(wuq), full(gkv), full(wukv)],
        out_specs=out_specs,
        out_shape=out_shape,
        scratch_shapes=[pltpu.VMEM((1, LANES), F32)],
        compiler_params=_cparams(("parallel", "arbitrary")),
        name="inproj",
    )(x, cin, cs, gmix, win, bblk, gq, wuq, gkv, wukv)


def _sub_mask(g, sub):
    lane = lax.broadcasted_iota(jnp.int32, (1, PAIR), 1)
    payload = (lane < LANES) & (lane // FOX_HEAD_DIM == sub)
    side = (lane >= LANES) & ((lane - LANES) // BIAS_SLOTS == 2 * g + sub)
    return payload | side


def _select_head(q2, g, sub):
    return jnp.where(_sub_mask(g, sub), q2.astype(F32), 0.0).astype(BF16)


def _attn_kernel(q_ref, k_ref, v_ref, o_ref, *, t, n_sub, chunk):
    g = pl.program_id(1)
    i = pl.program_id(2)
    q2 = q_ref[0]
    row = lax.broadcasted_iota(jnp.int32, (t, t), 0)
    col = lax.broadcasted_iota(jnp.int32, (t, t), 1)
    allowed = (col // chunk) <= (row // chunk)
    kd = k_ref[0, pl.ds(pl.multiple_of(i * t, t), t), :]
    vd = v_ref[0, pl.ds(pl.multiple_of(i * t, t), t), :]

    accs = []
    for sub in range(n_sub):
        qs = _select_head(q2, g, sub) if n_sub > 1 else q2
        s = jnp.where(allowed, _dot_nt(qs, kd), -1e30)
        m = jnp.max(s, axis=1, keepdims=True)
        acc = _dot(jnp.exp(s - m).astype(BF16), vd)

        def body(j, carry, qs=qs):
            m, acc = carry
            off = pl.multiple_of(j * t, t)
            s = _dot_nt(qs, k_ref[0, pl.ds(off, t), :])
            m_new = jnp.maximum(m, jnp.max(s, axis=1, keepdims=True))
            p = jnp.exp(s - m_new)
            acc = jnp.exp(m - m_new) * acc + _dot(p.astype(BF16), v_ref[0, pl.ds(off, t), :])
            return m_new, acc

        _, acc = lax.fori_loop(0, i, body, (m, acc))
        accs.append(acc)

    if n_sub > 1:
        first = lax.broadcasted_iota(jnp.int32, (1, LANES), 1) < FOX_HEAD_DIM
        num = jnp.where(first, accs[0][:, :LANES], accs[1][:, :LANES])
        den = jnp.where(first, accs[0][:, LANES:], accs[1][:, LANES:])
    else:
        num, den = accs[0][:, :LANES], accs[0][:, LANES:]
    o_ref[0] = (num / den).astype(o_ref.dtype)


def _attention(q2, k2, v2, *, t, n_sub, chunk):
    b, s, _ = q2.shape
    groups = q2.shape[2] // PAIR
    return pl.pallas_call(
        functools.partial(_attn_kernel, t=t, n_sub=n_sub, chunk=chunk),
        grid=(b, groups, s // t),
        in_specs=[pl.BlockSpec((1, t, PAIR), lambda bi, g, i: (bi, i, g)),
                  pl.BlockSpec((1, s, PAIR), lambda bi, g, i: (bi, 0, g)),
                  pl.BlockSpec((1, s, PAIR), lambda bi, g, i: (bi, 0, g))],
        out_specs=pl.BlockSpec((1, t, LANES), lambda bi, g, i: (bi, i, g)),
        out_shape=jax.ShapeDtypeStruct((b, s, groups * LANES), BF16),
        compiler_params=_cparams(("parallel", "parallel", "arbitrary")),
        name="attn_fox" if n_sub > 1 else "attn_mla",
    )(q2, k2, v2)


def _cache_scan_kernel(lfx_ref, kbc_ref, tot_ref, carry_ref):
    @pl.when(pl.program_id(1) == 0)
    def _():
        carry_ref[...] = jnp.zeros_like(carry_ref)

    cum = _cumsum_rows(lfx_ref[0], carry_ref[...])
    n = cum.shape[0]
    carry_ref[...] = cum[n - 1:n]
    tot_ref[0] = cum[n - 1:n]
    _, kb = _bias_blocks(cum)
    kbc_ref[0] = kb


def _cache_scan(lfx, *, tm):
    b, s, _ = lfx.shape
    return pl.pallas_call(
        _cache_scan_kernel,
        grid=(b, s // tm),
        in_specs=[pl.BlockSpec((1, tm, LANES), lambda i, j: (i, j, 0))],
        out_specs=(pl.BlockSpec((1, tm, LANES), lambda i, j: (i, j, 0)),
                   pl.BlockSpec((1, 1, LANES), lambda i, j: (i, 0, 0))),
        out_shape=(jax.ShapeDtypeStruct((b, s, LANES), BF16), jax.ShapeDtypeStruct((b, 1, LANES), F32)),
        scratch_shapes=[pltpu.VMEM((1, LANES), F32)],
        compiler_params=_cparams(("parallel", "arbitrary")),
        name="cache_scan",
    )(lfx)


def _softmax_pv(s_c, s_n, v_c, v_n):
    m = jnp.maximum(jnp.max(s_c, axis=1, keepdims=True), jnp.max(s_n, axis=1, keepdims=True))
    p_c = jnp.exp(s_c - m)
    p_n = jnp.exp(s_n - m)
    den = jnp.sum(p_c, axis=1, keepdims=True) + jnp.sum(p_n, axis=1, keepdims=True)
    return (_dot(p_c.astype(BF16), v_c) + _dot(p_n.astype(BF16), v_n)) / den


def _sample_attn_kernel(kc_ref, vc_ref, kbc_ref, ckvc_ref, krc_ref, wukv_ref,
                        qf2_ref, kf2_ref, vf2_ref, qm2_ref, km2_ref, vm2_ref,
                        fox_ref, mla_ref, *, past, chunk):
    n = qf2_ref.shape[1]
    row = lax.broadcasted_iota(jnp.int32, (n, n), 0)
    col = lax.broadcasted_iota(jnp.int32, (n, n), 1)
    chunk_ok = ((past + col) // chunk) <= ((past + row) // chunk)
    row2 = lax.broadcasted_iota(jnp.int32, (2 * n, n), 0) % n
    frame_ok2 = lax.broadcasted_iota(jnp.int32, (2 * n, n), 1) <= row2
    first = lax.broadcasted_iota(jnp.int32, (1, LANES), 1) < FOX_HEAD_DIM
    kbc = kbc_ref[0]

    for p in range(FOX_HEADS // 2):
        kk = jnp.concatenate([kc_ref[0, :, p * LANES:(p + 1) * LANES].astype(BF16), kbc], axis=1)
        vv = vc_ref[0, :, p * LANES:(p + 1) * LANES].astype(BF16)
        qpair = qf2_ref[0, :, p * PAIR:(p + 1) * PAIR]
        q_st = jnp.concatenate([_select_head(qpair, p, sub) for sub in range(2)], axis=0)
        s_c = _dot_nt(q_st, kk)
        s_n = _dot_nt(q_st, kf2_ref[0, :, p * PAIR:(p + 1) * PAIR])
        s_n = jnp.where(frame_ok2, s_n, -1e30)
        o = _softmax_pv(s_c, s_n, vv, vf2_ref[0, :, p * PAIR:p * PAIR + LANES])
        fox_ref[0, :, p * LANES:(p + 1) * LANES] = jnp.where(first, o[:n], o[n:]).astype(fox_ref.dtype)

    kv_c = _dot(ckvc_ref[0].astype(BF16), wukv_ref[...])
    krc = krc_ref[0].astype(BF16)
    krc = jnp.concatenate([krc, jnp.zeros_like(krc)], axis=1)
    for hd in range(MLA_HEADS):
        kk = jnp.concatenate([kv_c[:, hd * LANES:(hd + 1) * LANES].astype(BF16), krc], axis=1)
        vv = kv_c[:, MLA_HEADS * LANES + hd * LANES:MLA_HEADS * LANES + (hd + 1) * LANES].astype(BF16)
        q = qm2_ref[0, :, hd * PAIR:(hd + 1) * PAIR]
        s_c = _dot_nt(q, kk)
        s_n = jnp.where(chunk_ok, _dot_nt(q, km2_ref[0, :, hd * PAIR:(hd + 1) * PAIR]), -1e30)
        o = _softmax_pv(s_c, s_n, vv, vm2_ref[0, :, hd * PAIR:hd * PAIR + LANES])
        mla_ref[0, :, hd * LANES:(hd + 1) * LANES] = o.astype(mla_ref.dtype)


def _sample_attention(kc, vc, kbc, ckvc, krc, wukv, qf2, kf2, vf2, qm2, km2, vm2, *, chunk):
    b, past, _ = kc.shape
    n = qf2.shape[1]
    per_b = lambda a: pl.BlockSpec((1,) + a.shape[1:], lambda i: (i, 0, 0))
    return pl.pallas_call(
        functools.partial(_sample_attn_kernel, past=past, chunk=chunk),
        grid=(b,),
        in_specs=[per_b(kc), per_b(vc), per_b(kbc), per_b(ckvc), per_b(krc),
                  pl.BlockSpec(wukv.shape, lambda i: (0, 0)),
                  per_b(qf2), per_b(kf2), per_b(vf2), per_b(qm2), per_b(km2), per_b(vm2)],
        out_specs=(pl.BlockSpec((1, n, FOX_WIDTH), lambda i: (i, 0, 0)),
                   pl.BlockSpec((1, n, MLA_WIDTH), lambda i: (i, 0, 0))),
        out_shape=(jax.ShapeDtypeStruct((b, n, FOX_WIDTH), BF16), jax.ShapeDtypeStruct((b, n, MLA_WIDTH), BF16)),
        compiler_params=_cparams(("parallel",)),
        name="sample_attn",
    )(kc, vc, kbc, ckvc, krc, wukv, qf2, kf2, vf2, qm2, km2, vm2)


def _memkv_kernel(mem_ref, g_ref, wk_ref, wv_ref, mk_ref, mv_ref, mkb_ref, mvb_ref):
    m = _rms(mem_ref[0], g_ref[0]).astype(BF16)
    mk = _dot(m, wk_ref[0])
    mv = _dot(m, wv_ref[0])
    mk_ref[0, 0] = mk
    mv_ref[0, 0] = mv
    mkb_ref[0, 0] = mk.astype(BF16)
    mvb_ref[0, 0] = mv.astype(BF16)


def _memory_kv(mem, g, wk, wv):
    b, n, d = mem.shape
    depth = g.shape[0]
    o_spec = pl.BlockSpec((1, 1, n, d), lambda l, i: (l, i, 0, 0))
    w_spec = pl.BlockSpec((1, d, d), lambda l, i: (l, 0, 0))
    return pl.pallas_call(
        _memkv_kernel,
        grid=(depth, b),
        in_specs=[pl.BlockSpec((1, n, d), lambda l, i: (i, 0, 0)),
                  pl.BlockSpec((1, 1, d), lambda l, i: (l, 0, 0)), w_spec, w_spec],
        out_specs=(o_spec,) * 4,
        out_shape=(jax.ShapeDtypeStruct((depth, b, n, d), F32),) * 2 + (jax.ShapeDtypeStruct((depth, b, n, d), BF16),) * 2,
        compiler_params=_cparams(("arbitrary", "arbitrary")),
        name="memory_kv",
    )(mem, g, wk, wv)


def _post_kernel(x_ref, fox_ref, mla_ref, wo_ref, g_ref, wxq_ref, mk_ref, mv_ref, wxo_ref, o_ref):
    x = x_ref[0] + _dot(fox_ref[0], wo_ref[:FOX_WIDTH, :]) + _dot(mla_ref[0], wo_ref[FOX_WIDTH:, :])
    hc = _rms(x, g_ref[...]).astype(BF16)
    q = (_dot(hc, wxq_ref[...]) * (X_HEAD_DIM ** -0.5)).astype(BF16)
    mk = mk_ref[0].astype(BF16)
    mv = mv_ref[0].astype(BF16)
    outs = []
    for hd in range(X_HEADS):
        sl = slice(hd * X_HEAD_DIM, (hd + 1) * X_HEAD_DIM)
        s = _dot_nt(q[:, sl], mk[:, sl])
        p = jnp.exp(s - jnp.max(s, axis=1, keepdims=True))
        den = jnp.sum(p, axis=1, keepdims=True)
        outs.append((_dot(p.astype(BF16), mv[:, sl]) / den).astype(BF16))
    o_ref[0] = x + _dot(jnp.concatenate(outs, axis=1), wxo_ref[...])


def _post(x, fox, mla, wo, g, wxq, mk, mv, wxo, *, tm):
    b, s, d = x.shape
    tok = lambda w: pl.BlockSpec((1, tm, w), lambda i, j: (i, j, 0))
    full = lambda a: pl.BlockSpec(a.shape, lambda i, j: (0,) * a.ndim)
    mem = pl.BlockSpec((1,) + mk.shape[1:], lambda i, j: (i, 0, 0))
    return pl.pallas_call(
        _post_kernel,
        grid=(b, s // tm),
        in_specs=[tok(d), tok(FOX_WIDTH), tok(MLA_WIDTH), full(wo), full(g), full(wxq), mem, mem, full(wxo)],
        out_specs=tok(d),
        out_shape=jax.ShapeDtypeStruct((b, s, d), F32),
        compiler_params=_cparams(("parallel", "arbitrary")),
        name="post",
    )(x, fox, mla, wo, g, wxq, mk, mv, wxo)


def _mlp_kernel(x_ref, g_ref, wup_ref, wdn_ref, gf_ref, o_ref, *, final, ff_blk):
    x = x_ref[...]
    hm = _rms(x, g_ref[...]).astype(BF16)
    acc = x
    for c in range(D_FF // ff_blk):
        u = jnp.maximum(_dot(hm, wup_ref[:, c * ff_blk:(c + 1) * ff_blk]), 0.0)
        acc = acc + _dot(jnp.square(u).astype(BF16), wdn_ref[c * ff_blk:(c + 1) * ff_blk, :])
    o_ref[...] = _rms(acc, gf_ref[...]) if final else acc


def _mlp(x, g, wup, wdn, gf, *, final, tm):
    n, d = x.shape
    full = lambda a: pl.BlockSpec(a.shape, lambda i: (0,) * a.ndim, pipeline_mode=pl.Buffered(1))
    return pl.pallas_call(
        functools.partial(_mlp_kernel, final=final, ff_blk=1024),
        grid=(n // tm,),
        in_specs=[pl.BlockSpec((tm, d), lambda i: (i, 0)), full(g), full(wup), full(wdn), full(gf)],
        out_specs=pl.BlockSpec((tm, d), lambda i: (i, 0)),
        out_shape=jax.ShapeDtypeStruct((n, d), F32),
        compiler_params=_cparams(("parallel",)),
        name="mlp",
    )(x, g, wup, wdn, gf)


def _layout_win(w, b_forget):
    o = 3 * FOX_WIDTH
    g = w[:, o:o + FOX_HEADS]
    cq = w[:, o + FOX_HEADS:o + FOX_HEADS + MLA_Q_RANK]
    o2 = o + FOX_HEADS + MLA_Q_RANK
    ckv = w[:, o2:o2 + MLA_KV_RANK]
    kr = w[:, o2 + MLA_KV_RANK:]
    half = MLA_ROPE_DIM // 2
    kr_sw = jnp.concatenate([kr[:, half:], kr[:, :half]], axis=1)

    def gblock(a):
        pad = jnp.zeros(a.shape[:-1] + (LANES - FOX_HEADS * BIAS_SLOTS - FOX_HEADS,), a.dtype)
        return jnp.concatenate([jnp.repeat(a, BIAS_SLOTS, axis=-1), a, pad], axis=-1)

    wp = jnp.concatenate([w[:, :o], cq, ckv, kr, kr_sw, gblock(g)], axis=1).astype(BF16)
    return wp, gblock(b_forget[None, :])


def _layout_wuq(w):
    w = w.reshape(MLA_Q_RANK, MLA_HEADS, MLA_NOPE_DIM + MLA_ROPE_DIM)
    nope = w[:, :, :MLA_NOPE_DIM].reshape(MLA_Q_RANK, MLA_HEADS * MLA_NOPE_DIM)
    r = w[:, :, MLA_NOPE_DIM:]
    half = MLA_ROPE_DIM // 2
    r2 = jnp.concatenate([r, r[:, :, half:], r[:, :, :half]], axis=2).reshape(MLA_Q_RANK, MLA_HEADS * LANES)
    return jnp.concatenate([nope, r2], axis=1).astype(BF16)


def _layout_wukv(w):
    w = w.reshape(MLA_KV_RANK, MLA_HEADS, MLA_NOPE_DIM + MLA_V_DIM)
    return jnp.concatenate([w[:, :, :MLA_NOPE_DIM].reshape(MLA_KV_RANK, -1),
                            w[:, :, MLA_NOPE_DIM:].reshape(MLA_KV_RANK, -1)], axis=1).astype(BF16)


def _rope_table(pos):
    half = MLA_ROPE_DIM // 2
    inv = ROPE_THETA ** (-jnp.arange(half, dtype=F32) / half)
    ang = pos.astype(F32)[:, None] * inv[None, :]
    c, s = jnp.cos(ang), jnp.sin(ang)
    return jnp.concatenate([c, c, -s, s], axis=1)


def _tile(n, pref):
    return pref if n % pref == 0 else n


def kernel(x_prompt, x_sample, mem_prompt, cache_fox_k, cache_fox_v, cache_fox_logf, cache_mla_ckv, cache_mla_krope,
           cache_mem_k, cache_mem_v, norm_mix, w_in, b_forget, mla_q_norm, w_uq, mla_kv_norm, w_ukv, w_out,
           norm_cross, norm_mem, w_xq, w_mk, w_mv, w_xo, norm_mlp, w_up, w_down, norm_final):
    depth = w_in.shape[0]
    bp, sp, d = x_prompt.shape
    bs, ss, _ = x_sample.shape
    past = cache_fox_k.shape[2]
    n_mem = mem_prompt.shape[1]
    row = lambda a: a.reshape(1, -1)

    cs_p = _rope_table(jnp.arange(sp))
    cs_s = _rope_table(past + jnp.arange(ss))
    mk_all, mv_all, mkb_all, mvb_all = _memory_kv(mem_prompt, norm_mem[:, None, :], w_mk.astype(BF16), w_mv.astype(BF16))
    zero_carry = jnp.zeros((bp, 1, LANES), F32)
    gblock_pad = jnp.zeros(cache_fox_logf.shape[:-1] + (LANES - FOX_HEADS * BIAS_SLOTS,), F32)
    lfx_all = jnp.concatenate([jnp.repeat(cache_fox_logf.astype(F32), BIAS_SLOTS, axis=-1), gblock_pad], axis=-1)

    tm_p = _tile(sp, 512)
    t_attn = _tile(sp, 512)
    xp, xs = x_prompt, x_sample
    rows_p, rows_s = [], []
    for l in range(depth):
        win, bblk = _layout_win(w_in[l], b_forget[l])
        wuq, wukv = _layout_wuq(w_uq[l]), _layout_wukv(w_ukv[l])
        wo, wxq, wxo = w_out[l].astype(BF16), w_xq[l].astype(BF16), w_xo[l].astype(BF16)
        wup, wdn = w_up[l].astype(BF16), w_down[l].astype(BF16)
        gmix, gq, gkv, gx, gm = row(norm_mix[l]), row(mla_q_norm[l]), row(mla_kv_norm[l]), row(norm_cross[l]), row(norm_mlp[l])
        final = l == depth - 1

        kf, vf, lf, ckv, kr, qf2, kf2, vf2, qm2, km2, vm2 = _inproj(
            xp, zero_carry, cs_p, gmix, win, bblk, gq, wuq, gkv, wukv, tm=tm_p)
        fox = _attention(qf2, kf2, vf2, t=t_attn, n_sub=2, chunk=1)
        mla = _attention(qm2, km2, vm2, t=t_attn, n_sub=1, chunk=CHUNK)
        xp = _post(xp, fox, mla, wo, gx, wxq, mkb_all[l], mvb_all[l], wxo, tm=tm_p)
        xp = _mlp(xp.reshape(bp * sp, d), gm, wup, wdn, row(norm_final), final=final, tm=_tile(bp * sp, 512)).reshape(bp, sp, d)
        rows_p.append((kf, vf, lf, ckv, kr))

        kbc, tot = _cache_scan(lfx_all[l], tm=_tile(past, 512))
        kf, vf, lf, ckv, kr, qf2, kf2, vf2, qm2, km2, vm2 = _inproj(
            xs, tot, cs_s, gmix, win, bblk, gq, wuq, gkv, wukv, tm=ss)
        fox, mla = _sample_attention(
            cache_fox_k[l].reshape(bs, past, FOX_WIDTH), cache_fox_v[l].reshape(bs, past, FOX_WIDTH), kbc,
            cache_mla_ckv[l], cache_mla_krope[l], wukv, qf2, kf2, vf2, qm2, km2, vm2, chunk=CHUNK)
        xs = _post(xs, fox, mla, wo, gx, wxq, cache_mem_k[l].reshape(bs, n_mem, d), cache_mem_v[l].reshape(bs, n_mem, d),
                   wxo, tm=ss)
        xs = _mlp(xs.reshape(bs * ss, d), gm, wup, wdn, row(norm_final), final=final, tm=_tile(bs * ss, 512)).reshape(bs, ss, d)
        rows_s.append((kf, vf, lf, ckv, kr))

    def rows_out(rows, b, s):
        st = lambda i: jnp.stack([r[i] for r in rows])
        lo = FOX_HEADS * BIAS_SLOTS
        return (st(0).reshape(depth, b, s, FOX_HEADS, FOX_HEAD_DIM), st(1).reshape(depth, b, s, FOX_HEADS, FOX_HEAD_DIM),
                st(2)[..., lo:lo + FOX_HEADS], st(3), st(4))

    mem_shape = (depth, bp, n_mem, X_HEADS, X_HEAD_DIM)
    return ((xp, xs) + rows_out(rows_p, bp, sp) + (mk_all.reshape(mem_shape), mv_all.reshape(mem_shape))
            + rows_out(rows_s, bs, ss))
```

```python
import functools

import jax
import jax.numpy as jnp
from jax import lax
from jax.experimental import pallas as pl
from jax.experimental.pallas import tpu as pltpu

F32 = jnp.float32
BF16 = jnp.bfloat16

D_MODEL = 1024
CHUNK = 64
FOX_HEADS = 8
FOX_HEAD_DIM = 64
FOX_WIDTH = FOX_HEADS * FOX_HEAD_DIM
MLA_HEADS = 4
MLA_NOPE_DIM = 128
MLA_ROPE_DIM = 64
MLA_V_DIM = 128
MLA_WIDTH = MLA_HEADS * MLA_V_DIM
MLA_Q_RANK = 256
MLA_KV_RANK = 128
ROPE_THETA = 10000.0
X_HEADS = 4
X_HEAD_DIM = 256
D_FF = 4 * D_MODEL
EPS = 1e-6

LANES = 128
BF16_ROWS = 16
PAIR = 2 * LANES
BIAS_SLOTS = 8
VMEM_LIMIT = 56 * 1024 * 1024
FOX_V_ROWS = FOX_HEAD_DIM + BF16_ROWS
MLA_V_ROWS = MLA_V_DIM + BF16_ROWS

_OQ, _OK, _OV = 0, FOX_WIDTH, 2 * FOX_WIDTH
_OCQ = 3 * FOX_WIDTH
_OCKV = _OCQ + MLA_Q_RANK
_OKR = _OCKV + MLA_KV_RANK
_OG = _OKR + LANES
_RK, _RCKV, _RKR, _RG = 0, FOX_WIDTH, FOX_WIDTH + MLA_KV_RANK, FOX_WIDTH + MLA_KV_RANK + LANES
_TQ, _TK, _TV = 0, FOX_WIDTH, 2 * FOX_WIDTH
_TCQ = 3 * FOX_WIDTH
_TCKV = _TCQ + MLA_Q_RANK
_TKR = _TCKV + MLA_KV_RANK


def _cparams(sem):
    return pltpu.CompilerParams(dimension_semantics=sem, vmem_limit_bytes=VMEM_LIMIT)


def _rms(x, g):
    return x * lax.rsqrt(jnp.mean(jnp.square(x), axis=-1, keepdims=True) + EPS) * g


def _rms_rows(x, g):
    return x * lax.rsqrt(jnp.mean(jnp.square(x), axis=0, keepdims=True) + EPS) * g


def _dot(a, b):
    return jnp.dot(a, b, preferred_element_type=F32)


def _dot_nt(a, b):
    return lax.dot_general(a, b, (((1,), (1,)), ((), ())), preferred_element_type=F32)


def _split3(x):
    x1 = x.astype(BF16)
    r1 = x - x1.astype(F32)
    x2 = r1.astype(BF16)
    r2 = r1 - x2.astype(F32)
    return x1, x2, r2.astype(BF16)


def _cumsum_rows(x, carry):
    n = x.shape[0]
    blk = min(n, 256)
    r = lax.broadcasted_iota(jnp.int32, (blk, blk), 0)
    c = lax.broadcasted_iota(jnp.int32, (blk, blk), 1)
    tri = (c <= r).astype(BF16)
    outs = []
    for i in range(n // blk):
        x1, x2, x3 = _split3(x[i * blk:(i + 1) * blk])
        cb = _dot(tri, x1) + _dot(tri, x2) + _dot(tri, x3) + carry
        carry = cb[blk - 1:blk]
        outs.append(cb)
    return outs[0] if len(outs) == 1 else jnp.concatenate(outs, axis=0)


def _cumsum_lanes(x):
    n = x.shape[1]
    blk = min(n, 256)
    r = lax.broadcasted_iota(jnp.int32, (blk, blk), 0)
    c = lax.broadcasted_iota(jnp.int32, (blk, blk), 1)
    tri = (r <= c).astype(BF16)
    carry = jnp.zeros((x.shape[0], 1), F32)
    outs = []
    for i in range(n // blk):
        x1, x2, x3 = _split3(x[:, i * blk:(i + 1) * blk])
        cb = _dot(x1, tri) + _dot(x2, tri) + _dot(x3, tri) + carry
        carry = cb[:, blk - 1:blk]
        outs.append(cb)
    return outs[0] if len(outs) == 1 else jnp.concatenate(outs, axis=1)


def _bias_blocks(cum, slot_axis):
    c1, c2, c3 = (c.astype(F32) for c in _split3(cum))
    slot = lax.broadcasted_iota(jnp.int32, cum.shape, slot_axis)
    j = slot % BIAS_SLOTS
    valid = slot < FOX_HEADS * BIAS_SLOTS
    one = jnp.where(valid & (j < 6), 1.0, 0.0)
    qb = jnp.where(j == 0, c1, jnp.where(j == 1, c2, jnp.where(j == 2, c3, one)))
    kb = jnp.where(j < 3, one, jnp.where(j == 3, -c1, jnp.where(j == 4, -c2, jnp.where(j == 5, -c3, 0.0))))
    return jnp.where(valid, qb, 0.0), jnp.where(valid, kb, 0.0)


def _log_sigmoid(x):
    return jnp.minimum(x, 0.0) - jnp.log1p(jnp.exp(-jnp.abs(x)))


def _rope_pair(t, cs):
    u = t * cs
    return u + pltpu.roll(u, MLA_ROPE_DIM, 1)


def _rope_rows(x, cos, sin):
    half = MLA_ROPE_DIM // 2
    x1, x2 = x[:half], x[half:]
    return jnp.concatenate([x1 * cos - x2 * sin, x2 * cos + x1 * sin], axis=0)


def _inproj_kernel(x_ref, cin_ref, cs_ref, gmix_ref, win_ref, bblk_ref, gq_ref, wuq_ref, gkv_ref, wukv_ref,
                   kf_ref, vf_ref, lf_ref, ckv_ref, kr_ref,
                   qf2_ref, kf2_ref, vf2_ref, qm2_ref, km2_ref, vm2_ref, carry_ref):
    @pl.when(pl.program_id(1) == 0)
    def _():
        carry_ref[...] = cin_ref[0]

    x = x_ref[0]
    tm = x.shape[0]
    h = _rms(x, gmix_ref[...])
    z = _dot(h.astype(BF16), win_ref[...])
    cs = cs_ref[...]
    lane = lax.broadcasted_iota(jnp.int32, (1, LANES), 1)
    low = lane < MLA_ROPE_DIM
    ones_blk = jnp.ones((tm, LANES), BF16)

    kf_ref[0] = z[:, _OK:_OK + FOX_WIDTH]
    vf_ref[0] = z[:, _OV:_OV + FOX_WIDTH]

    lf = _log_sigmoid(z[:, _OG:_OG + LANES] + bblk_ref[...])
    lf_ref[0] = lf
    cum = _cumsum_rows(lf, carry_ref[...])
    carry_ref[...] = cum[tm - 1:tm]
    qb, kb = _bias_blocks(cum, 1)

    fox_scale = FOX_HEAD_DIM ** -0.5
    for p in range(FOX_HEADS // 2):
        lo, hi = p * PAIR, p * PAIR + LANES
        qf2_ref[0, :, lo:hi] = (z[:, _OQ + p * LANES:_OQ + (p + 1) * LANES] * fox_scale).astype(BF16)
        qf2_ref[0, :, hi:hi + LANES] = qb.astype(BF16)
        kf2_ref[0, :, lo:hi] = z[:, _OK + p * LANES:_OK + (p + 1) * LANES].astype(BF16)
        kf2_ref[0, :, hi:hi + LANES] = kb.astype(BF16)
        vf2_ref[0, :, lo:hi] = z[:, _OV + p * LANES:_OV + (p + 1) * LANES].astype(BF16)
        vf2_ref[0, :, hi:hi + LANES] = ones_blk

    kr = _rope_pair(z[:, _OKR:_OKR + LANES], cs)
    kr_ref[0] = kr[:, :MLA_ROPE_DIM]
    krb = jnp.where(low, kr, 0.0).astype(BF16)

    ckv = _rms(z[:, _OCKV:_OCKV + MLA_KV_RANK], gkv_ref[...])
    ckv_ref[0] = ckv
    kv = _dot(ckv.astype(BF16), wukv_ref[...])
    cqn = _rms(z[:, _OCQ:_OCQ + MLA_Q_RANK], gq_ref[...])
    qm = _dot(cqn.astype(BF16), wuq_ref[...])
    mla_scale = (MLA_NOPE_DIM + MLA_ROPE_DIM) ** -0.5
    for hd in range(MLA_HEADS):
        lo, hi = hd * PAIR, hd * PAIR + LANES
        qr = _rope_pair(qm[:, MLA_HEADS * LANES + hd * LANES:MLA_HEADS * LANES + (hd + 1) * LANES], cs)
        qm2_ref[0, :, lo:hi] = (qm[:, hd * LANES:(hd + 1) * LANES] * mla_scale).astype(BF16)
        qm2_ref[0, :, hi:hi + LANES] = (jnp.where(low, qr, 0.0) * mla_scale).astype(BF16)
        km2_ref[0, :, lo:hi] = kv[:, hd * LANES:(hd + 1) * LANES].astype(BF16)
        km2_ref[0, :, hi:hi + LANES] = krb
        vm2_ref[0, :, lo:hi] = kv[:, MLA_HEADS * LANES + hd * LANES:MLA_HEADS * LANES + (hd + 1) * LANES].astype(BF16)
        vm2_ref[0, :, hi:hi + LANES] = ones_blk


def _inproj(x, cin, cs, gmix, win, bblk, gq, wuq, gkv, wukv, *, tm):
    b, s, _ = x.shape
    grid = (b, s // tm)
    tok = lambda w: pl.BlockSpec((1, tm, w), lambda i, j: (i, j, 0))
    full = lambda a: pl.BlockSpec(a.shape, lambda i, j: (0,) * a.ndim)
    out_shape = (
        jax.ShapeDtypeStruct((b, s, FOX_WIDTH), F32), jax.ShapeDtypeStruct((b, s, FOX_WIDTH), F32),
        jax.ShapeDtypeStruct((b, s, LANES), F32), jax.ShapeDtypeStruct((b, s, MLA_KV_RANK), F32),
        jax.ShapeDtypeStruct((b, s, MLA_ROPE_DIM), F32),
    ) + tuple(jax.ShapeDtypeStruct((b, s, 4 * PAIR), BF16) for _ in range(6))
    out_specs = (tok(FOX_WIDTH), tok(FOX_WIDTH), tok(LANES), tok(MLA_KV_RANK), tok(MLA_ROPE_DIM)) + (tok(4 * PAIR),) * 6
    return pl.pallas_call(
        _inproj_kernel,
        grid=grid,
        in_specs=[tok(D_MODEL), pl.BlockSpec((1, 1, LANES), lambda i, j: (i, 0, 0)),
                  pl.BlockSpec((tm, LANES), lambda i, j: (j, 0)),
                  full(gmix), full(win), full(bblk), full(gq), full(wuq), full(gkv), full(wukv)],
        out_specs=out_specs,
        out_shape=out_shape,
        scratch_shapes=[pltpu.VMEM((1, LANES), F32)],
        compiler_params=_cparams(("parallel", "arbitrary")),
        name="inproj_sample",
    )(x, cin, cs, gmix, win, bblk, gq, wuq, gkv, wukv)


def _inproj_prompt_kernel(*refs, n_prev):
    (x_ref, cs_ref, cos_ref, sin_ref, gmix_ref, wr_ref, wt_ref, bblk_ref, gq_ref, wuqt_ref, gkv_ref, gkvc_ref,
     wnope_ref, wvt_ref) = refs[:14]
    prev_k_ref, prev_v_ref = (refs[14], refs[15]) if n_prev else (None, None)
    (kt_ref, vt_ref, lft_ref, ckv_ref, krt_ref,
     qt_ref, k2_ref, vt2_ref, qmt_ref, km2_ref, vmt_ref, carry_ref) = refs[14 + (2 if n_prev else 0):]

    @pl.when(pl.program_id(1) == 0)
    def _():
        carry_ref[...] = jnp.zeros_like(carry_ref)

    x = x_ref[0]
    tm = x.shape[0]
    hb = _rms(x, gmix_ref[...]).astype(BF16)
    zr = _dot(hb, wr_ref[...])
    zt = _dot_nt(wt_ref[...], hb)
    cs, cos, sin = cs_ref[...], cos_ref[...], sin_ref[...]
    ones_rows = jnp.ones((BF16_ROWS, tm), BF16)

    if n_prev:
        kt_ref[:n_prev, 0] = prev_k_ref[:, 0]
        vt_ref[:n_prev, 0] = prev_v_ref[:, 0]
    kt_ref[n_prev, 0] = zt[_TK:_TK + FOX_WIDTH]
    vt_ref[n_prev, 0] = zt[_TV:_TV + FOX_WIDTH]

    lf = _log_sigmoid(zr[:, _RG:_RG + LANES] + bblk_ref[...])
    cum = _cumsum_rows(lf, carry_ref[...])
    carry_ref[...] = cum[tm - 1:tm]
    qb, kb = _bias_blocks(cum, 1)
    kb = kb.astype(BF16)
    qbt = qb.T.astype(BF16)
    lo = FOX_HEADS * BIAS_SLOTS
    lft_ref[0] = lf.T[lo:lo + FOX_HEADS]

    fox_scale = FOX_HEAD_DIM ** -0.5
    for p in range(FOX_HEADS // 2):
        qt_ref[0, 0, p * PAIR:p * PAIR + LANES] = (zt[_TQ + p * LANES:_TQ + (p + 1) * LANES] * fox_scale).astype(BF16)
        qt_ref[0, 0, p * PAIR + LANES:(p + 1) * PAIR] = qbt
        k2_ref[0, :, p * PAIR:p * PAIR + LANES] = zr[:, _RK + p * LANES:_RK + (p + 1) * LANES].astype(BF16)
        k2_ref[0, :, p * PAIR + LANES:(p + 1) * PAIR] = kb
    for hd in range(FOX_HEADS):
        r0 = hd * FOX_V_ROWS
        vt2_ref[0, 0, r0:r0 + FOX_HEAD_DIM] = zt[_TV + hd * FOX_HEAD_DIM:_TV + (hd + 1) * FOX_HEAD_DIM].astype(BF16)
        vt2_ref[0, 0, r0 + FOX_HEAD_DIM:r0 + FOX_V_ROWS] = ones_rows

    krt_ref[0] = _rope_rows(zt[_TKR:_TKR + MLA_ROPE_DIM], cos, sin)
    kr = _rope_pair(zr[:, _RKR:_RKR + LANES], cs)
    krb = jnp.where(lax.broadcasted_iota(jnp.int32, (1, LANES), 1) < MLA_ROPE_DIM, kr, 0.0).astype(BF16)
    ckv = _rms(zr[:, _RCKV:_RCKV + MLA_KV_RANK], gkv_ref[...])
    ckv_ref[0] = ckv
    knope = _dot(ckv.astype(BF16), wnope_ref[...])
    ckvt = _rms_rows(zt[_TCKV:_TCKV + MLA_KV_RANK], gkvc_ref[...])
    vmt = _dot(wvt_ref[...], ckvt.astype(BF16))
    cqnt = _rms_rows(zt[_TCQ:_TCQ + MLA_Q_RANK], gq_ref[...])
    qmt = _dot(wuqt_ref[...], cqnt.astype(BF16))
    mla_scale = (MLA_NOPE_DIM + MLA_ROPE_DIM) ** -0.5
    zeros_rows = jnp.zeros((PAIR - MLA_NOPE_DIM - MLA_ROPE_DIM, tm), BF16)
    for hd in range(MLA_HEADS):
        q0 = hd * PAIR
        qmt_ref[0, 0, q0:q0 + MLA_NOPE_DIM] = (qmt[hd * MLA_NOPE_DIM:(hd + 1) * MLA_NOPE_DIM] * mla_scale).astype(BF16)
        r0 = MLA_HEADS * MLA_NOPE_DIM + hd * MLA_ROPE_DIM
        qmt_ref[0, 0, q0 + MLA_NOPE_DIM:q0 + MLA_NOPE_DIM + MLA_ROPE_DIM] = (
            _rope_rows(qmt[r0:r0 + MLA_ROPE_DIM], cos, sin) * mla_scale).astype(BF16)
        qmt_ref[0, 0, q0 + MLA_NOPE_DIM + MLA_ROPE_DIM:q0 + PAIR] = zeros_rows
        km2_ref[0, :, q0:q0 + LANES] = knope[:, hd * LANES:(hd + 1) * LANES].astype(BF16)
        km2_ref[0, :, q0 + LANES:q0 + PAIR] = krb
        v0 = hd * MLA_V_ROWS
        vmt_ref[0, 0, v0:v0 + MLA_V_DIM] = vmt[hd * MLA_V_DIM:(hd + 1) * MLA_V_DIM].astype(BF16)
        vmt_ref[0, 0, v0 + MLA_V_DIM:v0 + MLA_V_ROWS] = ones_rows


def _inproj_prompt(x, cs, cos_t, sin_t, gmix, wr, wt, bblk, gq_col, wuqt, gkv, gkv_col, wnope, wvt, prev_k, prev_v, *, tm):
    b, s, _ = x.shape
    nt = s // tm
    n_prev = 0 if prev_k is None else prev_k.shape[0]
    full = lambda a: pl.BlockSpec(a.shape, lambda i, j: (0,) * a.ndim, pipeline_mode=pl.Buffered(1))
    tile4 = lambda rows: pl.BlockSpec((1, 1, rows, tm), lambda i, j: (i, j, 0, 0))
    tok = lambda w: pl.BlockSpec((1, tm, w), lambda i, j: (i, j, 0))
    feat = lambda rows: pl.BlockSpec((1, rows, tm), lambda i, j: (i, 0, j))
    stacked = lambda n: pl.BlockSpec((n, 1, FOX_WIDTH, tm), lambda i, j: (0, i, 0, j))
    ins = [x, cs, cos_t, sin_t, gmix, wr, wt, bblk, gq_col, wuqt, gkv, gkv_col, wnope, wvt]
    in_specs = [tok(D_MODEL), pl.BlockSpec((tm, LANES), lambda i, j: (j, 0)),
                pl.BlockSpec((cos_t.shape[0], tm), lambda i, j: (0, j)),
                pl.BlockSpec((sin_t.shape[0], tm), lambda i, j: (0, j))] + [full(a) for a in ins[4:]]
    if n_prev:
        ins += [prev_k, prev_v]
        in_specs += [stacked(n_prev), stacked(n_prev)]
    op4 = lambda rows: jax.ShapeDtypeStruct((b, nt, rows, tm), BF16)
    out_shape = (jax.ShapeDtypeStruct((n_prev + 1, b, FOX_WIDTH, s), F32),) * 2 + (
        jax.ShapeDtypeStruct((b, FOX_HEADS, s), F32), jax.ShapeDtypeStruct((b, s, MLA_KV_RANK), F32),
        jax.ShapeDtypeStruct((b, MLA_ROPE_DIM, s), F32),
        op4(4 * PAIR), jax.ShapeDtypeStruct((b, s, 4 * PAIR), BF16), op4(FOX_HEADS * FOX_V_ROWS),
        op4(4 * PAIR), jax.ShapeDtypeStruct((b, s, 4 * PAIR), BF16), op4(MLA_HEADS * MLA_V_ROWS))
    out_specs = (stacked(n_prev + 1),) * 2 + (
        feat(FOX_HEADS), tok(MLA_KV_RANK), feat(MLA_ROPE_DIM),
        tile4(4 * PAIR), tok(4 * PAIR), tile4(FOX_HEADS * FOX_V_ROWS),
        tile4(4 * PAIR), tok(4 * PAIR), tile4(MLA_HEADS * MLA_V_ROWS))
    return pl.pallas_call(
        functools.partial(_inproj_prompt_kernel, n_prev=n_prev),
        grid=(b, nt),
        in_specs=in_specs,
        out_specs=out_specs,
        out_shape=out_shape,
        scratch_shapes=[pltpu.VMEM((1, LANES), F32)],
        compiler_params=_cparams(("parallel", "arbitrary")),
        name="inproj_prompt",
    )(*ins)


def _select_head_rows(qt, g, sub):
    r = lax.broadcasted_iota(jnp.int32, qt.shape, 0)
    keep = ((r < LANES) & (r // FOX_HEAD_DIM == sub)) | ((r >= LANES) & ((r - LANES) // BIAS_SLOTS == 2 * g + sub))
    return jnp.where(keep, qt.astype(F32), 0.0).astype(BF16)


def _attn_kernel(qt_ref, k_ref, vt_ref, o_ref, sa_ref, sb_ref, m_ref, acc_ref, *, t, gps, n_sub, chunk, v_rows, v_dim):
    g0 = pl.program_id(1) * gps
    i = pl.program_id(2)
    key = lax.broadcasted_iota(jnp.int32, (t, t), 0)
    qry = lax.broadcasted_iota(jnp.int32, (t, t), 1)
    allowed = (key // chunk) <= (qry // chunk)
    qts = []
    for g in range(gps):
        qt = qt_ref[0, 0, g * PAIR:(g + 1) * PAIR]
        qts += [_select_head_rows(qt, g0 + g, sub) if n_sub > 1 else qt for sub in range(n_sub)]

    n_chain = gps * n_sub

    def scores(blk, dst_ref, diagonal=False):
        off = pl.multiple_of(blk * t, t)
        for c in range(n_chain):
            st = _dot(k_ref[0, pl.ds(off, t), (c // n_sub) * PAIR:(c // n_sub + 1) * PAIR], qts[c])
            dst_ref[c] = jnp.where(allowed, st, -1e30) if diagonal else st

    def consume(blk, src_ref):
        for c in range(n_chain):
            st = src_ref[c]
            m = m_ref[c]
            m_new = jnp.maximum(m, jnp.max(st, axis=0, keepdims=True))
            pt = jnp.exp(st - m_new).astype(BF16)
            acc_ref[c] = jnp.exp(m - m_new) * acc_ref[c] + _dot(vt_ref[0, blk, c * v_rows:(c + 1) * v_rows], pt)
            m_ref[c] = m_new

    m_ref[...] = jnp.full(m_ref.shape, -1e30, F32)
    acc_ref[...] = jnp.zeros(acc_ref.shape, F32)
    scores(i, sa_ref, diagonal=True)
    n_pairs = i // 2

    def pair(p, carry):
        scores(2 * p, sb_ref)
        consume(jnp.where(p == 0, i, 2 * p - 1), sa_ref)
        scores(2 * p + 1, sa_ref)
        consume(2 * p, sb_ref)
        return carry

    lax.fori_loop(0, n_pairs, pair, 0)
    odd = i % 2 == 1

    @pl.when(odd)
    def _():
        scores(i - 1, sb_ref)

    consume(jnp.where(n_pairs == 0, i, 2 * n_pairs - 1), sa_ref)

    @pl.when(odd)
    def _():
        consume(i - 1, sb_ref)

    for g in range(gps):
        outs = []
        for sub in range(n_sub):
            acc = acc_ref[g * n_sub + sub]
            outs.append(acc[:v_dim] / acc[v_dim:v_dim + 1])
        ot = outs[0] if n_sub == 1 else jnp.concatenate(outs, axis=0)
        o_ref[0, :, g * LANES:(g + 1) * LANES] = ot.T.astype(o_ref.dtype)


def _attention(qt, k2, vt, *, gps, n_sub, chunk, v_rows, v_dim):
    b, nt, _, t = qt.shape
    s = nt * t
    steps = 4 // gps
    vr = gps * n_sub * v_rows
    return pl.pallas_call(
        functools.partial(_attn_kernel, t=t, gps=gps, n_sub=n_sub, chunk=chunk, v_rows=v_rows, v_dim=v_dim),
        grid=(b, steps, nt),
        in_specs=[pl.BlockSpec((1, 1, gps * PAIR, t), lambda bi, g, i: (bi, i, g, 0)),
                  pl.BlockSpec((1, s, gps * PAIR), lambda bi, g, i: (bi, 0, g)),
                  pl.BlockSpec((1, nt, vr, t), lambda bi, g, i: (bi, 0, g, 0))],
        out_specs=pl.BlockSpec((1, t, gps * LANES), lambda bi, g, i: (bi, i, g)),
        out_shape=jax.ShapeDtypeStruct((b, s, 4 * LANES), BF16),
        scratch_shapes=[pltpu.VMEM((gps * n_sub, t, t), F32), pltpu.VMEM((gps * n_sub, t, t), F32),
                        pltpu.VMEM((gps * n_sub, 1, t), F32), pltpu.VMEM((gps * n_sub, v_rows, t), F32)],
        compiler_params=_cparams(("parallel", "parallel", "arbitrary")),
        name="attn_fox" if n_sub > 1 else "attn_mla",
    )(qt, k2, vt)


def _cache_scan_kernel(lft_ref, kbct_ref, tot_ref):
    lf = lft_ref[0, 0]
    heads, n = lf.shape
    cum = _cumsum_lanes(jnp.concatenate([lf, jnp.zeros_like(lf)], axis=0))[:heads]
    tot_ref[0, 0] = jnp.broadcast_to(cum[:, n - 1:n], (heads, LANES))
    rep = jnp.concatenate([jnp.broadcast_to(cum[hd:hd + 1], (BIAS_SLOTS, n)) for hd in range(heads)]
                          + [jnp.zeros((LANES - heads * BIAS_SLOTS, n), F32)], axis=0)
    kbct_ref[0, 0] = _bias_blocks(rep, 0)[1].astype(BF16)


def _cache_scan(lft):
    depth, b, heads, n = lft.shape
    return pl.pallas_call(
        _cache_scan_kernel,
        grid=(depth, b),
        in_specs=[pl.BlockSpec((1, 1, heads, n), lambda l, i: (l, i, 0, 0))],
        out_specs=(pl.BlockSpec((1, 1, LANES, n), lambda l, i: (l, i, 0, 0)),
                   pl.BlockSpec((1, 1, heads, LANES), lambda l, i: (l, i, 0, 0))),
        out_shape=(jax.ShapeDtypeStruct((depth, b, LANES, n), BF16), jax.ShapeDtypeStruct((depth, b, heads, LANES), F32)),
        compiler_params=_cparams(("parallel", "parallel")),
        name="cache_scan",
    )(lft)


def _select_head(q2, g, sub):
    lane = lax.broadcasted_iota(jnp.int32, (1, PAIR), 1)
    payload = (lane < LANES) & (lane // FOX_HEAD_DIM == sub)
    side = (lane >= LANES) & ((lane - LANES) // BIAS_SLOTS == 2 * g + sub)
    return jnp.where(payload | side, q2.astype(F32), 0.0).astype(BF16)


def _softmax_pv(s_c, s_n, vt_c, v_n):
    m = jnp.maximum(jnp.max(s_c, axis=1, keepdims=True), jnp.max(s_n, axis=1, keepdims=True))
    p_c = jnp.exp(s_c - m)
    p_n = jnp.exp(s_n - m)
    den = jnp.sum(p_c, axis=1, keepdims=True) + jnp.sum(p_n, axis=1, keepdims=True)
    return (_dot_nt(p_c.astype(BF16), vt_c) + _dot(p_n.astype(BF16), v_n)) / den


def _sample_attn_kernel(kct_ref, vct_ref, kbct_ref, ckvc_ref, krct_ref, wukvt_ref,
                        qf2_ref, kf2_ref, vf2_ref, qm2_ref, km2_ref, vm2_ref,
                        fox_ref, mla_ref, *, past, chunk):
    n = qf2_ref.shape[1]
    row = lax.broadcasted_iota(jnp.int32, (n, n), 0)
    col = lax.broadcasted_iota(jnp.int32, (n, n), 1)
    chunk_ok = ((past + col) // chunk) <= ((past + row) // chunk)
    row2 = lax.broadcasted_iota(jnp.int32, (2 * n, n), 0) % n
    frame_ok2 = lax.broadcasted_iota(jnp.int32, (2 * n, n), 1) <= row2
    first = lax.broadcasted_iota(jnp.int32, (1, LANES), 1) < FOX_HEAD_DIM
    kbct = kbct_ref[0, 0]

    for p in range(FOX_HEADS // 2):
        kkt = jnp.concatenate([kct_ref[0, 0, p * LANES:(p + 1) * LANES].astype(BF16), kbct], axis=0)
        vvt = vct_ref[0, 0, p * LANES:(p + 1) * LANES].astype(BF16)
        qpair = qf2_ref[0, :, p * PAIR:(p + 1) * PAIR]
        q_st = jnp.concatenate([_select_head(qpair, p, sub) for sub in range(2)], axis=0)
        s_c = _dot(q_st, kkt)
        s_n = _dot_nt(q_st, kf2_ref[0, :, p * PAIR:(p + 1) * PAIR])
        s_n = jnp.where(frame_ok2, s_n, -1e30)
        o = _softmax_pv(s_c, s_n, vvt, vf2_ref[0, :, p * PAIR:p * PAIR + LANES])
        fox_ref[0, :, p * LANES:(p + 1) * LANES] = jnp.where(first, o[:n], o[n:]).astype(fox_ref.dtype)

    kvt_c = _dot_nt(wukvt_ref[...], ckvc_ref[0, 0].astype(BF16))
    krct = krct_ref[0, 0].astype(BF16)
    pad = jnp.zeros((PAIR - MLA_NOPE_DIM - MLA_ROPE_DIM, krct.shape[1]), BF16)
    for hd in range(MLA_HEADS):
        kkt = jnp.concatenate([kvt_c[hd * LANES:(hd + 1) * LANES].astype(BF16), krct, pad], axis=0)
        vvt = kvt_c[MLA_HEADS * LANES + hd * LANES:MLA_HEADS * LANES + (hd + 1) * LANES].astype(BF16)
        q = qm2_ref[0, :, hd * PAIR:(hd + 1) * PAIR]
        s_c = _dot(q, kkt)
        s_n = jnp.where(chunk_ok, _dot_nt(q, km2_ref[0, :, hd * PAIR:(hd + 1) * PAIR]), -1e30)
        o = _softmax_pv(s_c, s_n, vvt, vm2_ref[0, :, hd * PAIR:hd * PAIR + LANES])
        mla_ref[0, :, hd * LANES:(hd + 1) * LANES] = o.astype(mla_ref.dtype)


def _sample_attention(l, kct, vct, kbct, ckvc, krct, wukvt, qf2, kf2, vf2, qm2, km2, vm2, *, chunk):
    b, n = qf2.shape[:2]
    past = kct.shape[3]
    cache = lambda a: pl.BlockSpec((1, 1) + a.shape[2:], lambda i: (l, i, 0, 0))
    per_b = lambda a: pl.BlockSpec((1,) + a.shape[1:], lambda i: (i, 0, 0))
    return pl.pallas_call(
        functools.partial(_sample_attn_kernel, past=past, chunk=chunk),
        grid=(b,),
        in_specs=[cache(kct), cache(vct), cache(kbct), cache(ckvc), cache(krct),
                  pl.BlockSpec(wukvt.shape, lambda i: (0, 0)),
                  per_b(qf2), per_b(kf2), per_b(vf2), per_b(qm2), per_b(km2), per_b(vm2)],
        out_specs=(pl.BlockSpec((1, n, FOX_WIDTH), lambda i: (i, 0, 0)),
                   pl.BlockSpec((1, n, MLA_WIDTH), lambda i: (i, 0, 0))),
        out_shape=(jax.ShapeDtypeStruct((b, n, FOX_WIDTH), BF16), jax.ShapeDtypeStruct((b, n, MLA_WIDTH), BF16)),
        compiler_params=_cparams(("parallel",)),
        name="sample_attn",
    )(kct, vct, kbct, ckvc, krct, wukvt, qf2, kf2, vf2, qm2, km2, vm2)


def _memkv_kernel(mem_ref, g_ref, wk_ref, wv_ref, mk_ref, mv_ref, mkb_ref, mvb_ref):
    m = _rms(mem_ref[0], g_ref[0]).astype(BF16)
    mk = _dot(m, wk_ref[0])
    mv = _dot(m, wv_ref[0])
    mk_ref[0, 0] = mk
    mv_ref[0, 0] = mv
    mkb_ref[0, 0] = mk.astype(BF16)
    mvb_ref[0, 0] = mv.astype(BF16)


def _memory_kv(mem, g, wk, wv):
    b, n, d = mem.shape
    depth = g.shape[0]
    o_spec = pl.BlockSpec((1, 1, n, d), lambda l, i: (l, i, 0, 0))
    w_spec = pl.BlockSpec((1, d, d), lambda l, i: (l, 0, 0))
    return pl.pallas_call(
        _memkv_kernel,
        grid=(depth, b),
        in_specs=[pl.BlockSpec((1, n, d), lambda l, i: (i, 0, 0)),
                  pl.BlockSpec((1, 1, d), lambda l, i: (l, 0, 0)), w_spec, w_spec],
        out_specs=(o_spec,) * 4,
        out_shape=(jax.ShapeDtypeStruct((depth, b, n, d), F32),) * 2 + (jax.ShapeDtypeStruct((depth, b, n, d), BF16),) * 2,
        compiler_params=_cparams(("arbitrary", "arbitrary")),
        name="memory_kv",
    )(mem, g, wk, wv)


def _post_kernel(x_ref, fox_ref, mla_ref, wo_ref, g_ref, wxq_ref, mk_ref, mv_ref, wxo_ref, o_ref):
    x = x_ref[0] + _dot(fox_ref[0], wo_ref[:FOX_WIDTH, :]) + _dot(mla_ref[0], wo_ref[FOX_WIDTH:, :])
    hc = _rms(x, g_ref[...]).astype(BF16)
    q = (_dot(hc, wxq_ref[...]) * (X_HEAD_DIM ** -0.5)).astype(BF16)
    by_head = len(mk_ref.shape) == 5
    outs = []
    for hd in range(X_HEADS):
        sl = slice(hd * X_HEAD_DIM, (hd + 1) * X_HEAD_DIM)
        mk = mk_ref[0, 0, :, hd, :] if by_head else mk_ref[0, 0, :, sl]
        mv = mv_ref[0, 0, :, hd, :] if by_head else mv_ref[0, 0, :, sl]
        s = _dot_nt(q[:, sl], mk.astype(BF16))
        p = jnp.exp(s - jnp.max(s, axis=1, keepdims=True))
        den = jnp.sum(p, axis=1, keepdims=True)
        outs.append((_dot(p.astype(BF16), mv.astype(BF16)) / den).astype(BF16))
    o_ref[0] = x + _dot(jnp.concatenate(outs, axis=1), wxo_ref[...])


def _post(l, x, fox, mla, wo, g, wxq, mk, mv, wxo, *, tm):
    b, s, d = x.shape
    tok = lambda w: pl.BlockSpec((1, tm, w), lambda i, j: (i, j, 0))
    full = lambda a: pl.BlockSpec(a.shape, lambda i, j: (0,) * a.ndim)
    mem = pl.BlockSpec((1, 1) + mk.shape[2:], lambda i, j: (l, i) + (0,) * (mk.ndim - 2))
    return pl.pallas_call(
        _post_kernel,
        grid=(b, s // tm),
        in_specs=[tok(d), tok(FOX_WIDTH), tok(MLA_WIDTH), full(wo), full(g), full(wxq), mem, mem, full(wxo)],
        out_specs=tok(d),
        out_shape=jax.ShapeDtypeStruct((b, s, d), F32),
        compiler_params=_cparams(("parallel", "arbitrary")),
        name="post",
    )(x, fox, mla, wo, g, wxq, mk, mv, wxo)


def _mlp_kernel(x_ref, g_ref, wup_ref, wdn_ref, gf_ref, o_ref, *, final, ff_blk):
    x = x_ref[...]
    hm = _rms(x, g_ref[...]).astype(BF16)
    acc = x
    for c in range(D_FF // ff_blk):
        u = jnp.maximum(_dot(hm, wup_ref[:, c * ff_blk:(c + 1) * ff_blk]), 0.0)
        acc = acc + _dot(jnp.square(u).astype(BF16), wdn_ref[c * ff_blk:(c + 1) * ff_blk, :])
    o_ref[...] = _rms(acc, gf_ref[...]) if final else acc


def _mlp(x, g, wup, wdn, gf, *, final, tm):
    n, d = x.shape
    full = lambda a: pl.BlockSpec(a.shape, lambda i: (0,) * a.ndim, pipeline_mode=pl.Buffered(1))
    return pl.pallas_call(
        functools.partial(_mlp_kernel, final=final, ff_blk=1024),
        grid=(n // tm,),
        in_specs=[pl.BlockSpec((tm, d), lambda i: (i, 0)), full(g), full(wup), full(wdn), full(gf)],
        out_specs=pl.BlockSpec((tm, d), lambda i: (i, 0)),
        out_shape=jax.ShapeDtypeStruct((n, d), F32),
        compiler_params=_cparams(("parallel",)),
        name="mlp",
    )(x, g, wup, wdn, gf)


def _gate_block(a):
    pad = jnp.zeros(a.shape[:-1] + (LANES - FOX_HEADS * BIAS_SLOTS - FOX_HEADS,), a.dtype)
    return jnp.concatenate([jnp.repeat(a, BIAS_SLOTS, axis=-1), a, pad], axis=-1)


def _split_win(w):
    o = 3 * FOX_WIDTH
    o2 = o + FOX_HEADS + MLA_Q_RANK
    half = MLA_ROPE_DIM // 2
    kr = w[:, o2 + MLA_KV_RANK:]
    return dict(q=w[:, :FOX_WIDTH], k=w[:, FOX_WIDTH:2 * FOX_WIDTH], v=w[:, 2 * FOX_WIDTH:o], g=w[:, o:o + FOX_HEADS],
                cq=w[:, o + FOX_HEADS:o2], ckv=w[:, o2:o2 + MLA_KV_RANK], kr=kr,
                kr_sw=jnp.concatenate([kr[:, half:], kr[:, :half]], axis=1))


def _layout_win_sample(w):
    p = _split_win(w)
    return jnp.concatenate([p["q"], p["k"], p["v"], p["cq"], p["ckv"], p["kr"], p["kr_sw"], _gate_block(p["g"])], axis=1).astype(BF16)


def _layout_win_prompt(w):
    p = _split_win(w)
    wr = jnp.concatenate([p["k"], p["ckv"], p["kr"], p["kr_sw"], _gate_block(p["g"])], axis=1).astype(BF16)
    wt = jnp.concatenate([p["q"], p["k"], p["v"], p["cq"], p["ckv"], p["kr"]], axis=1).T.astype(BF16)
    return wr, wt


def _layout_wuq_sample(w):
    w = w.reshape(MLA_Q_RANK, MLA_HEADS, MLA_NOPE_DIM + MLA_ROPE_DIM)
    nope = w[:, :, :MLA_NOPE_DIM].reshape(MLA_Q_RANK, MLA_HEADS * MLA_NOPE_DIM)
    r = w[:, :, MLA_NOPE_DIM:]
    half = MLA_ROPE_DIM // 2
    r2 = jnp.concatenate([r, r[:, :, half:], r[:, :, :half]], axis=2).reshape(MLA_Q_RANK, MLA_HEADS * LANES)
    return jnp.concatenate([nope, r2], axis=1).astype(BF16)


def _layout_wuq_prompt(w):
    w = w.reshape(MLA_Q_RANK, MLA_HEADS, MLA_NOPE_DIM + MLA_ROPE_DIM)
    return jnp.concatenate([w[:, :, :MLA_NOPE_DIM].reshape(MLA_Q_RANK, -1),
                            w[:, :, MLA_NOPE_DIM:].reshape(MLA_Q_RANK, -1)], axis=1).T.astype(BF16)


def _layout_wukv(w):
    w = w.reshape(MLA_KV_RANK, MLA_HEADS, MLA_NOPE_DIM + MLA_V_DIM)
    return jnp.concatenate([w[:, :, :MLA_NOPE_DIM].reshape(MLA_KV_RANK, -1),
                            w[:, :, MLA_NOPE_DIM:].reshape(MLA_KV_RANK, -1)], axis=1).astype(BF16)


def _rope_tables(pos):
    half = MLA_ROPE_DIM // 2
    inv = ROPE_THETA ** (-jnp.arange(half, dtype=F32) / half)
    ang = pos.astype(F32)[:, None] * inv[None, :]
    return jnp.cos(ang), jnp.sin(ang)


def _tile(n, pref):
    return pref if n % pref == 0 else n


def kernel(x_prompt, x_sample, mem_prompt, cache_fox_k, cache_fox_v, cache_fox_logf, cache_mla_ckv, cache_mla_krope,
           cache_mem_k, cache_mem_v, norm_mix, w_in, b_forget, mla_q_norm, w_uq, mla_kv_norm, w_ukv, w_out,
           norm_cross, norm_mem, w_xq, w_mk, w_mv, w_xo, norm_mlp, w_up, w_down, norm_final):
    depth = w_in.shape[0]
    bp, sp, d = x_prompt.shape
    bs, ss, _ = x_sample.shape
    past = cache_fox_k.shape[2]
    n_mem = mem_prompt.shape[1]
    row = lambda a: a.reshape(1, -1)
    col = lambda a: a.reshape(-1, 1)

    cos_p, sin_p = _rope_tables(jnp.arange(sp))
    cos_s, sin_s = _rope_tables(past + jnp.arange(ss))
    cs_p = jnp.concatenate([cos_p, cos_p, -sin_p, sin_p], axis=1)
    cs_s = jnp.concatenate([cos_s, cos_s, -sin_s, sin_s], axis=1)
    cos_pt, sin_pt = cos_p.T, sin_p.T

    mk_all, mv_all, mkb_all, mvb_all = _memory_kv(mem_prompt, norm_mem[:, None, :], w_mk.astype(BF16), w_mv.astype(BF16))

    kct = jnp.transpose(cache_fox_k, (0, 1, 3, 4, 2)).reshape(depth, bs, FOX_WIDTH, past)
    vct = jnp.transpose(cache_fox_v, (0, 1, 3, 4, 2)).reshape(depth, bs, FOX_WIDTH, past)
    krct = jnp.transpose(cache_mla_krope, (0, 1, 3, 2))
    kbct, tot = _cache_scan(jnp.transpose(cache_fox_logf, (0, 1, 3, 2)).astype(F32))
    cin_all = _gate_block(tot[..., 0])[:, :, None, :]

    tm_p = _tile(sp, 512)
    xp, xs = x_prompt, x_sample
    kt_all = vt_all = None
    rows_p, rows_s = [], []
    for l in range(depth):
        wsplit_s = _layout_win_sample(w_in[l])
        wr, wt = _layout_win_prompt(w_in[l])
        bblk = _gate_block(b_forget[l][None, :])
        wukv = _layout_wukv(w_ukv[l])
        wnope, wvt = wukv[:, :MLA_HEADS * MLA_NOPE_DIM], wukv[:, MLA_HEADS * MLA_NOPE_DIM:].T
        wo, wxq, wxo = w_out[l].astype(BF16), w_xq[l].astype(BF16), w_xo[l].astype(BF16)
        wup, wdn = w_up[l].astype(BF16), w_down[l].astype(BF16)
        gmix, gq, gkv, gx, gm = row(norm_mix[l]), row(mla_q_norm[l]), row(mla_kv_norm[l]), row(norm_cross[l]), row(norm_mlp[l])
        final = l == depth - 1

        kt_all, vt_all, lft, ckv, krt, qt, k2, vt2, qmt, km2, vmt = _inproj_prompt(
            xp, cs_p, cos_pt, sin_pt, gmix, wr, wt, bblk, col(mla_q_norm[l]), _layout_wuq_prompt(w_uq[l]),
            gkv, col(mla_kv_norm[l]), wnope, wvt, kt_all, vt_all, tm=tm_p)
        fox = _attention(qt, k2, vt2, gps=1, n_sub=2, chunk=1, v_rows=FOX_V_ROWS, v_dim=FOX_HEAD_DIM)
        mla = _attention(qmt, km2, vmt, gps=2, n_sub=1, chunk=CHUNK, v_rows=MLA_V_ROWS, v_dim=MLA_V_DIM)
        xp = _post(l, xp, fox, mla, wo, gx, wxq, mkb_all, mvb_all, wxo, tm=tm_p)
        xp = _mlp(xp.reshape(bp * sp, d), gm, wup, wdn, row(norm_final), final=final, tm=_tile(bp * sp, 512)).reshape(bp, sp, d)
        rows_p.append((lft, ckv, krt))

        kf, vf, lf, ckv, kr, qf2, kf2, vf2, qm2, km2, vm2 = _inproj(
            xs, cin_all[l], cs_s, gmix, wsplit_s, bblk, gq, _layout_wuq_sample(w_uq[l]), gkv, wukv, tm=ss)
        fox, mla = _sample_attention(l, kct, vct, kbct, cache_mla_ckv, krct, wukv.T, qf2, kf2, vf2, qm2, km2, vm2, chunk=CHUNK)
        xs = _post(l, xs, fox, mla, wo, gx, wxq, cache_mem_k, cache_mem_v, wxo, tm=ss)
        xs = _mlp(xs.reshape(bs * ss, d), gm, wup, wdn, row(norm_final), final=final, tm=_tile(bs * ss, 512)).reshape(bs, ss, d)
        rows_s.append((kf, vf, lf, ckv, kr))

    st = lambda rows, i: jnp.stack([r[i] for r in rows])
    heads5 = lambda a: jnp.transpose(a.reshape(depth, bp, FOX_HEADS, FOX_HEAD_DIM, sp), (0, 1, 4, 2, 3))
    lo = FOX_HEADS * BIAS_SLOTS
    mem_shape = (depth, bp, n_mem, X_HEADS, X_HEAD_DIM)
    return (xp, xs,
            heads5(kt_all), heads5(vt_all), jnp.transpose(st(rows_p, 0), (0, 1, 3, 2)), st(rows_p, 1),
            jnp.transpose(st(rows_p, 2), (0, 1, 3, 2)),
            mk_all.reshape(mem_shape), mv_all.reshape(mem_shape),
            st(rows_s, 0).reshape(depth, bs, ss, FOX_HEADS, FOX_HEAD_DIM), st(rows_s, 1).reshape(depth, bs, ss, FOX_HEADS, FOX_HEAD_DIM),
            st(rows_s, 2)[..., lo:lo + FOX_HEADS], st(rows_s, 3), st(rows_s, 4))
```

```python
import functools

import jax
import jax.numpy as jnp
from jax import lax
from jax.experimental import pallas as pl
from jax.experimental.pallas import tpu as pltpu

F32 = jnp.float32
BF16 = jnp.bfloat16

D_MODEL = 1024
CHUNK = 64
FOX_HEADS = 8
FOX_HEAD_DIM = 64
FOX_WIDTH = FOX_HEADS * FOX_HEAD_DIM
MLA_HEADS = 4
MLA_NOPE_DIM = 128
MLA_ROPE_DIM = 64
MLA_V_DIM = 128
MLA_WIDTH = MLA_HEADS * MLA_V_DIM
MLA_Q_RANK = 256
MLA_KV_RANK = 128
ROPE_THETA = 10000.0
X_HEADS = 4
X_HEAD_DIM = 256
D_FF = 4 * D_MODEL
EPS = 1e-6

LOG2E = 1.4426950408889634
LANES = 128
BF16_ROWS = 16
PAIR = 2 * LANES
BIAS_SLOTS = 8
VMEM_LIMIT = 56 * 1024 * 1024
FOX_V_ROWS = FOX_HEAD_DIM + BF16_ROWS
MLA_V_ROWS = MLA_V_DIM + BF16_ROWS

_OQ, _OK, _OV = 0, FOX_WIDTH, 2 * FOX_WIDTH
_OCQ = 3 * FOX_WIDTH
_OCKV = _OCQ + MLA_Q_RANK
_OKR = _OCKV + MLA_KV_RANK
_OG = _OKR + LANES
_RK, _RCKV, _RKR, _RG = 0, FOX_WIDTH, FOX_WIDTH + MLA_KV_RANK, FOX_WIDTH + MLA_KV_RANK + LANES
_TQ, _TK, _TV = 0, FOX_WIDTH, 2 * FOX_WIDTH
_TCQ = 3 * FOX_WIDTH
_TCKV = _TCQ + MLA_Q_RANK
_TKR = _TCKV + MLA_KV_RANK


def _cparams(sem):
    return pltpu.CompilerParams(dimension_semantics=sem, vmem_limit_bytes=VMEM_LIMIT)


def _rms(x, g):
    return x * lax.rsqrt(jnp.mean(jnp.square(x), axis=-1, keepdims=True) + EPS) * g


def _rms_rows(x, g):
    return x * lax.rsqrt(jnp.mean(jnp.square(x), axis=0, keepdims=True) + EPS) * g


def _dot(a, b):
    return jnp.dot(a, b, preferred_element_type=F32)


def _dot_nt(a, b):
    return lax.dot_general(a, b, (((1,), (1,)), ((), ())), preferred_element_type=F32)


def _split3(x):
    x1 = x.astype(BF16)
    r1 = x - x1.astype(F32)
    x2 = r1.astype(BF16)
    r2 = r1 - x2.astype(F32)
    return x1, x2, r2.astype(BF16)


def _cumsum_rows(x, carry):
    n = x.shape[0]
    blk = min(n, 256)
    r = lax.broadcasted_iota(jnp.int32, (blk, blk), 0)
    c = lax.broadcasted_iota(jnp.int32, (blk, blk), 1)
    tri = (c <= r).astype(BF16)
    outs = []
    for i in range(n // blk):
        x1, x2, x3 = _split3(x[i * blk:(i + 1) * blk])
        cb = _dot(tri, x1) + _dot(tri, x2) + _dot(tri, x3) + carry
        carry = cb[blk - 1:blk]
        outs.append(cb)
    return outs[0] if len(outs) == 1 else jnp.concatenate(outs, axis=0)


def _cumsum_lanes(x):
    n = x.shape[1]
    blk = min(n, 256)
    r = lax.broadcasted_iota(jnp.int32, (blk, blk), 0)
    c = lax.broadcasted_iota(jnp.int32, (blk, blk), 1)
    tri = (r <= c).astype(BF16)
    carry = jnp.zeros((x.shape[0], 1), F32)
    outs = []
    for i in range(n // blk):
        x1, x2, x3 = _split3(x[:, i * blk:(i + 1) * blk])
        cb = _dot(x1, tri) + _dot(x2, tri) + _dot(x3, tri) + carry
        carry = cb[:, blk - 1:blk]
        outs.append(cb)
    return outs[0] if len(outs) == 1 else jnp.concatenate(outs, axis=1)


def _bias_blocks(cum, slot_axis):
    c1, c2, c3 = (c.astype(F32) for c in _split3(cum))
    slot = lax.broadcasted_iota(jnp.int32, cum.shape, slot_axis)
    j = slot % BIAS_SLOTS
    valid = slot < FOX_HEADS * BIAS_SLOTS
    one = jnp.where(valid & (j < 6), 1.0, 0.0)
    qb = jnp.where(j == 0, c1, jnp.where(j == 1, c2, jnp.where(j == 2, c3, one)))
    kb = jnp.where(j < 3, one, jnp.where(j == 3, -c1, jnp.where(j == 4, -c2, jnp.where(j == 5, -c3, 0.0))))
    return jnp.where(valid, qb, 0.0), jnp.where(valid, kb, 0.0)


def _log_sigmoid(x):
    return jnp.minimum(x, 0.0) - jnp.log1p(jnp.exp(-jnp.abs(x)))


def _rope_pair(t, cs):
    u = t * cs
    return u + pltpu.roll(u, MLA_ROPE_DIM, 1)


def _rope_rows(x, cos, sin):
    half = MLA_ROPE_DIM // 2
    x1, x2 = x[:half], x[half:]
    return jnp.concatenate([x1 * cos - x2 * sin, x2 * cos + x1 * sin], axis=0)


def _inproj_kernel(x_ref, cin_ref, cs_ref, gmix_ref, win_ref, bblk_ref, gq_ref, wuq_ref, gkv_ref, wukv_ref,
                   kf_ref, vf_ref, lf_ref, ckv_ref, kr_ref,
                   qf2_ref, kf2_ref, vf2_ref, qm2_ref, km2_ref, vm2_ref, carry_ref):
    @pl.when(pl.program_id(1) == 0)
    def _():
        carry_ref[...] = cin_ref[0]

    x = x_ref[0]
    tm = x.shape[0]
    h = _rms(x, gmix_ref[...])
    z = _dot(h.astype(BF16), win_ref[...])
    cs = cs_ref[...]
    lane = lax.broadcasted_iota(jnp.int32, (1, LANES), 1)
    low = lane < MLA_ROPE_DIM
    ones_blk = jnp.ones((tm, LANES), BF16)

    kf_ref[0] = z[:, _OK:_OK + FOX_WIDTH]
    vf_ref[0] = z[:, _OV:_OV + FOX_WIDTH]

    lf = _log_sigmoid(z[:, _OG:_OG + LANES] + bblk_ref[...])
    lf_ref[0] = lf
    cum = _cumsum_rows(lf, carry_ref[...])
    carry_ref[...] = cum[tm - 1:tm]
    qb, kb = _bias_blocks(cum, 1)

    fox_scale = FOX_HEAD_DIM ** -0.5
    for p in range(FOX_HEADS // 2):
        lo, hi = p * PAIR, p * PAIR + LANES
        qf2_ref[0, :, lo:hi] = (z[:, _OQ + p * LANES:_OQ + (p + 1) * LANES] * fox_scale).astype(BF16)
        qf2_ref[0, :, hi:hi + LANES] = qb.astype(BF16)
        kf2_ref[0, :, lo:hi] = z[:, _OK + p * LANES:_OK + (p + 1) * LANES].astype(BF16)
        kf2_ref[0, :, hi:hi + LANES] = kb.astype(BF16)
        vf2_ref[0, :, lo:hi] = z[:, _OV + p * LANES:_OV + (p + 1) * LANES].astype(BF16)
        vf2_ref[0, :, hi:hi + LANES] = ones_blk

    kr = _rope_pair(z[:, _OKR:_OKR + LANES], cs)
    kr_ref[0] = kr[:, :MLA_ROPE_DIM]
    krb = jnp.where(low, kr, 0.0).astype(BF16)

    ckv = _rms(z[:, _OCKV:_OCKV + MLA_KV_RANK], gkv_ref[...])
    ckv_ref[0] = ckv
    kv = _dot(ckv.astype(BF16), wukv_ref[...])
    cqn = _rms(z[:, _OCQ:_OCQ + MLA_Q_RANK], gq_ref[...])
    qm = _dot(cqn.astype(BF16), wuq_ref[...])
    mla_scale = (MLA_NOPE_DIM + MLA_ROPE_DIM) ** -0.5
    for hd in range(MLA_HEADS):
        lo, hi = hd * PAIR, hd * PAIR + LANES
        qr = _rope_pair(qm[:, MLA_HEADS * LANES + hd * LANES:MLA_HEADS * LANES + (hd + 1) * LANES], cs)
        qm2_ref[0, :, lo:hi] = (qm[:, hd * LANES:(hd + 1) * LANES] * mla_scale).astype(BF16)
        qm2_ref[0, :, hi:hi + LANES] = (jnp.where(low, qr, 0.0) * mla_scale).astype(BF16)
        km2_ref[0, :, lo:hi] = kv[:, hd * LANES:(hd + 1) * LANES].astype(BF16)
        km2_ref[0, :, hi:hi + LANES] = krb
        vm2_ref[0, :, lo:hi] = kv[:, MLA_HEADS * LANES + hd * LANES:MLA_HEADS * LANES + (hd + 1) * LANES].astype(BF16)
        vm2_ref[0, :, hi:hi + LANES] = ones_blk


def _inproj(x, cin, cs, gmix, win, bblk, gq, wuq, gkv, wukv, *, tm):
    b, s, _ = x.shape
    grid = (b, s // tm)
    tok = lambda w: pl.BlockSpec((1, tm, w), lambda i, j: (i, j, 0))
    full = lambda a: pl.BlockSpec(a.shape, lambda i, j: (0,) * a.ndim)
    out_shape = (
        jax.ShapeDtypeStruct((b, s, FOX_WIDTH), F32), jax.ShapeDtypeStruct((b, s, FOX_WIDTH), F32),
        jax.ShapeDtypeStruct((b, s, LANES), F32), jax.ShapeDtypeStruct((b, s, MLA_KV_RANK), F32),
        jax.ShapeDtypeStruct((b, s, MLA_ROPE_DIM), F32),
    ) + tuple(jax.ShapeDtypeStruct((b, s, 4 * PAIR), BF16) for _ in range(6))
    out_specs = (tok(FOX_WIDTH), tok(FOX_WIDTH), tok(LANES), tok(MLA_KV_RANK), tok(MLA_ROPE_DIM)) + (tok(4 * PAIR),) * 6
    return pl.pallas_call(
        _inproj_kernel,
        grid=grid,
        in_specs=[tok(D_MODEL), pl.BlockSpec((1, 1, LANES), lambda i, j: (i, 0, 0)),
                  pl.BlockSpec((tm, LANES), lambda i, j: (j, 0)),
                  full(gmix), full(win), full(bblk), full(gq), full(wuq), full(gkv), full(wukv)],
        out_specs=out_specs,
        out_shape=out_shape,
        scratch_shapes=[pltpu.VMEM((1, LANES), F32)],
        compiler_params=_cparams(("parallel", "arbitrary")),
        name="inproj_sample",
    )(x, cin, cs, gmix, win, bblk, gq, wuq, gkv, wukv)


def _inproj_prompt_kernel(*refs, n_prev):
    (x_ref, cs_ref, cos_ref, sin_ref, gmix_ref, wr_ref, wt_ref, bblk_ref, gq_ref, wuqt_ref, gkv_ref, gkvc_ref,
     wnope_ref, wvt_ref) = refs[:14]
    prev_k_ref, prev_v_ref = (refs[14], refs[15]) if n_prev else (None, None)
    (kt_ref, vt_ref, lft_ref, ckv_ref, krt_ref,
     qt_ref, k2_ref, vt2_ref, qmt_ref, km2_ref, vmt_ref, carry_ref) = refs[14 + (2 if n_prev else 0):]

    @pl.when(pl.program_id(1) == 0)
    def _():
        carry_ref[...] = jnp.zeros_like(carry_ref)

    x = x_ref[0]
    tm = x.shape[0]
    hb = _rms(x, gmix_ref[...]).astype(BF16)
    zr = _dot(hb, wr_ref[...])
    zt = _dot_nt(wt_ref[...], hb)
    cs, cos, sin = cs_ref[...], cos_ref[...], sin_ref[...]
    ones_rows = jnp.ones((BF16_ROWS, tm), BF16)

    if n_prev:
        kt_ref[:n_prev, 0] = prev_k_ref[:, 0]
        vt_ref[:n_prev, 0] = prev_v_ref[:, 0]
    kt_ref[n_prev, 0] = zt[_TK:_TK + FOX_WIDTH]
    vt_ref[n_prev, 0] = zt[_TV:_TV + FOX_WIDTH]

    lf = _log_sigmoid(zr[:, _RG:_RG + LANES] + bblk_ref[...])
    cum = _cumsum_rows(lf, carry_ref[...])
    carry_ref[...] = cum[tm - 1:tm]
    qb, kb = _bias_blocks(cum * LOG2E, 1)
    kb = kb.astype(BF16)
    qbt = qb.T
    lo = FOX_HEADS * BIAS_SLOTS
    lft_ref[0] = lf.T[lo:lo + FOX_HEADS]

    fox_scale = FOX_HEAD_DIM ** -0.5 * LOG2E
    rowi = lax.broadcasted_iota(jnp.int32, (LANES, tm), 0)
    for p in range(FOX_HEADS // 2):
        zq = zt[_TQ + p * LANES:_TQ + (p + 1) * LANES] * fox_scale
        for sub in range(2):
            hd = 2 * p + sub
            qt_ref[0, 0, hd * PAIR:hd * PAIR + LANES] = jnp.where(rowi // FOX_HEAD_DIM == sub, zq, 0.0).astype(BF16)
            qt_ref[0, 0, hd * PAIR + LANES:(hd + 1) * PAIR] = jnp.where(rowi // BIAS_SLOTS == hd, qbt, 0.0).astype(BF16)
        k2_ref[0, :, p * PAIR:p * PAIR + LANES] = zr[:, _RK + p * LANES:_RK + (p + 1) * LANES].astype(BF16)
        k2_ref[0, :, p * PAIR + LANES:(p + 1) * PAIR] = kb
    for hd in range(FOX_HEADS):
        r0 = hd * FOX_V_ROWS
        vt2_ref[0, 0, r0:r0 + FOX_HEAD_DIM] = zt[_TV + hd * FOX_HEAD_DIM:_TV + (hd + 1) * FOX_HEAD_DIM].astype(BF16)
        vt2_ref[0, 0, r0 + FOX_HEAD_DIM:r0 + FOX_V_ROWS] = ones_rows

    krt_ref[0] = _rope_rows(zt[_TKR:_TKR + MLA_ROPE_DIM], cos, sin)
    kr = _rope_pair(zr[:, _RKR:_RKR + LANES], cs)
    krb = jnp.where(lax.broadcasted_iota(jnp.int32, (1, LANES), 1) < MLA_ROPE_DIM, kr, 0.0).astype(BF16)
    ckv = _rms(zr[:, _RCKV:_RCKV + MLA_KV_RANK], gkv_ref[...])
    ckv_ref[0] = ckv
    knope = _dot(ckv.astype(BF16), wnope_ref[...])
    ckvt = _rms_rows(zt[_TCKV:_TCKV + MLA_KV_RANK], gkvc_ref[...])
    vmt = _dot(wvt_ref[...], ckvt.astype(BF16))
    cqnt = _rms_rows(zt[_TCQ:_TCQ + MLA_Q_RANK], gq_ref[...])
    qmt = _dot(wuqt_ref[...], cqnt.astype(BF16))
    mla_scale = (MLA_NOPE_DIM + MLA_ROPE_DIM) ** -0.5 * LOG2E
    zeros_rows = jnp.zeros((PAIR - MLA_NOPE_DIM - MLA_ROPE_DIM, tm), BF16)
    for hd in range(MLA_HEADS):
        q0 = hd * PAIR
        qmt_ref[0, 0, q0:q0 + MLA_NOPE_DIM] = (qmt[hd * MLA_NOPE_DIM:(hd + 1) * MLA_NOPE_DIM] * mla_scale).astype(BF16)
        r0 = MLA_HEADS * MLA_NOPE_DIM + hd * MLA_ROPE_DIM
        qmt_ref[0, 0, q0 + MLA_NOPE_DIM:q0 + MLA_NOPE_DIM + MLA_ROPE_DIM] = (
            _rope_rows(qmt[r0:r0 + MLA_ROPE_DIM], cos, sin) * mla_scale).astype(BF16)
        qmt_ref[0, 0, q0 + MLA_NOPE_DIM + MLA_ROPE_DIM:q0 + PAIR] = zeros_rows
        km2_ref[0, :, q0:q0 + LANES] = knope[:, hd * LANES:(hd + 1) * LANES].astype(BF16)
        km2_ref[0, :, q0 + LANES:q0 + PAIR] = krb
        v0 = hd * MLA_V_ROWS
        vmt_ref[0, 0, v0:v0 + MLA_V_DIM] = vmt[hd * MLA_V_DIM:(hd + 1) * MLA_V_DIM].astype(BF16)
        vmt_ref[0, 0, v0 + MLA_V_DIM:v0 + MLA_V_ROWS] = ones_rows


def _inproj_prompt(x, cs, cos_t, sin_t, gmix, wr, wt, bblk, gq_col, wuqt, gkv, gkv_col, wnope, wvt, prev_k, prev_v, *, tm):
    b, s, _ = x.shape
    nt = s // tm
    n_prev = 0 if prev_k is None else prev_k.shape[0]
    full = lambda a: pl.BlockSpec(a.shape, lambda i, j: (0,) * a.ndim, pipeline_mode=pl.Buffered(1))
    tile4 = lambda rows: pl.BlockSpec((1, 1, rows, tm), lambda i, j: (i, j, 0, 0))
    tok = lambda w: pl.BlockSpec((1, tm, w), lambda i, j: (i, j, 0))
    feat = lambda rows: pl.BlockSpec((1, rows, tm), lambda i, j: (i, 0, j))
    stacked = lambda n: pl.BlockSpec((n, 1, FOX_WIDTH, tm), lambda i, j: (0, i, 0, j))
    ins = [x, cs, cos_t, sin_t, gmix, wr, wt, bblk, gq_col, wuqt, gkv, gkv_col, wnope, wvt]
    in_specs = [tok(D_MODEL), pl.BlockSpec((tm, LANES), lambda i, j: (j, 0)),
                pl.BlockSpec((cos_t.shape[0], tm), lambda i, j: (0, j)),
                pl.BlockSpec((sin_t.shape[0], tm), lambda i, j: (0, j))] + [full(a) for a in ins[4:]]
    if n_prev:
        ins += [prev_k, prev_v]
        in_specs += [stacked(n_prev), stacked(n_prev)]
    op4 = lambda rows: jax.ShapeDtypeStruct((b, nt, rows, tm), BF16)
    out_shape = (jax.ShapeDtypeStruct((n_prev + 1, b, FOX_WIDTH, s), F32),) * 2 + (
        jax.ShapeDtypeStruct((b, FOX_HEADS, s), F32), jax.ShapeDtypeStruct((b, s, MLA_KV_RANK), F32),
        jax.ShapeDtypeStruct((b, MLA_ROPE_DIM, s), F32),
        op4(FOX_HEADS * PAIR), jax.ShapeDtypeStruct((b, s, 4 * PAIR), BF16), op4(FOX_HEADS * FOX_V_ROWS),
        op4(MLA_HEADS * PAIR), jax.ShapeDtypeStruct((b, s, 4 * PAIR), BF16), op4(MLA_HEADS * MLA_V_ROWS))
    out_specs = (stacked(n_prev + 1),) * 2 + (
        feat(FOX_HEADS), tok(MLA_KV_RANK), feat(MLA_ROPE_DIM),
        tile4(FOX_HEADS * PAIR), tok(4 * PAIR), tile4(FOX_HEADS * FOX_V_ROWS),
        tile4(MLA_HEADS * PAIR), tok(4 * PAIR), tile4(MLA_HEADS * MLA_V_ROWS))
    return pl.pallas_call(
        functools.partial(_inproj_prompt_kernel, n_prev=n_prev),
        grid=(b, nt),
        in_specs=in_specs,
        out_specs=out_specs,
        out_shape=out_shape,
        scratch_shapes=[pltpu.VMEM((1, LANES), F32)],
        compiler_params=_cparams(("parallel", "arbitrary")),
        name="inproj_prompt",
    )(*ins)


def _attn_kernel(qt_ref, k_ref, vt_ref, o_ref, sa_ref, sb_ref, sd_ref, m_ref, acc_ref, *,
                 t, nt, n_chain, n_sub, chunk, v_rows, v_dim):
    i = pl.program_id(2)
    key = lax.broadcasted_iota(jnp.int32, (t, t), 0)
    qry = lax.broadcasted_iota(jnp.int32, (t, t), 1)
    allowed = (key // chunk) <= (qry // chunk)

    def scores(qblk, kblk, dst_ref, diagonal=False):
        off = pl.multiple_of(kblk * t, t)
        for c in range(n_chain):
            g = c // n_sub
            st = _dot(k_ref[0, pl.ds(off, t), g * PAIR:(g + 1) * PAIR], qt_ref[0, qblk, c * PAIR:(c + 1) * PAIR])
            dst_ref[c] = jnp.where(allowed, st, -1e30) if diagonal else st

    def consume(kblk, src_ref):
        for c in range(n_chain):
            st = src_ref[c]
            m = m_ref[c]
            m_new = jnp.maximum(m, jnp.max(st, axis=0, keepdims=True))
            pt = jnp.exp2(st - m_new).astype(BF16)
            acc_ref[c] = jnp.exp2(m - m_new) * acc_ref[c] + _dot(vt_ref[0, kblk, c * v_rows:(c + 1) * v_rows], pt)
            m_ref[c] = m_new

    def next_diagonal():
        nxt = jnp.minimum(i + 1, nt - 1)
        scores(nxt, nxt, sd_ref, diagonal=True)

    @pl.when(i == 0)
    def _():
        scores(0, 0, sd_ref, diagonal=True)

    m_ref[...] = jnp.full(m_ref.shape, -1e30, F32)
    acc_ref[...] = jnp.zeros(acc_ref.shape, F32)

    @pl.when(i == 0)
    def _():
        consume(0, sd_ref)
        next_diagonal()

    @pl.when(i > 0)
    def _():
        scores(i, 0, sa_ref)
        consume(i, sd_ref)
        n_pairs = (i - 1) // 2

        def pair(p, carry):
            scores(i, 2 * p + 1, sb_ref)
            consume(2 * p, sa_ref)
            scores(i, 2 * p + 2, sa_ref)
            consume(2 * p + 1, sb_ref)
            return carry

        lax.fori_loop(0, n_pairs, pair, 0)
        one_more = (i - 1) % 2 == 1

        @pl.when(one_more)
        def _():
            scores(i, i - 1, sb_ref)
            consume(2 * n_pairs, sa_ref)
            next_diagonal()
            consume(i - 1, sb_ref)

        @pl.when(jnp.logical_not(one_more))
        def _():
            next_diagonal()
            consume(2 * n_pairs, sa_ref)

    for g in range(n_chain // n_sub):
        outs = []
        for sub in range(n_sub):
            acc = acc_ref[g * n_sub + sub]
            outs.append(acc[:v_dim] / acc[v_dim:v_dim + 1])
        ot = outs[0] if n_sub == 1 else jnp.concatenate(outs, axis=0)
        o_ref[0, :, g * LANES:(g + 1) * LANES] = ot.T.astype(o_ref.dtype)


def _attention(qt, k2, vt, *, n_chain, n_sub, chunk, v_rows, v_dim):
    b, nt, rows, t = qt.shape
    s = nt * t
    steps = rows // (n_chain * PAIR)
    gps = n_chain // n_sub
    return pl.pallas_call(
        functools.partial(_attn_kernel, t=t, nt=nt, n_chain=n_chain, n_sub=n_sub, chunk=chunk, v_rows=v_rows, v_dim=v_dim),
        grid=(b, steps, nt),
        in_specs=[pl.BlockSpec((1, nt, n_chain * PAIR, t), lambda bi, g, i: (bi, 0, g, 0)),
                  pl.BlockSpec((1, s, gps * PAIR), lambda bi, g, i: (bi, 0, g)),
                  pl.BlockSpec((1, nt, n_chain * v_rows, t), lambda bi, g, i: (bi, 0, g, 0))],
        out_specs=pl.BlockSpec((1, t, gps * LANES), lambda bi, g, i: (bi, i, g)),
        out_shape=jax.ShapeDtypeStruct((b, s, 4 * LANES), BF16),
        scratch_shapes=[pltpu.VMEM((n_chain, t, t), F32)] * 3
                       + [pltpu.VMEM((n_chain, 1, t), F32), pltpu.VMEM((n_chain, v_rows, t), F32)],
        compiler_params=_cparams(("parallel", "arbitrary", "arbitrary")),
        name="attn_fox" if n_sub > 1 else "attn_mla",
    )(qt, k2, vt)


def _cache_scan_kernel(lft_ref, kbct_ref, tot_ref):
    lf = lft_ref[0, 0]
    heads, n = lf.shape
    cum = _cumsum_lanes(jnp.concatenate([lf, jnp.zeros_like(lf)], axis=0))[:heads]
    tot_ref[0, 0] = jnp.broadcast_to(cum[:, n - 1:n], (heads, LANES))
    rep = jnp.concatenate([jnp.broadcast_to(cum[hd:hd + 1], (BIAS_SLOTS, n)) for hd in range(heads)]
                          + [jnp.zeros((LANES - heads * BIAS_SLOTS, n), F32)], axis=0)
    kbct_ref[0, 0] = _bias_blocks(rep, 0)[1].astype(BF16)


def _cache_scan(lft):
    depth, b, heads, n = lft.shape
    return pl.pallas_call(
        _cache_scan_kernel,
        grid=(depth, b),
        in_specs=[pl.BlockSpec((1, 1, heads, n), lambda l, i: (l, i, 0, 0))],
        out_specs=(pl.BlockSpec((1, 1, LANES, n), lambda l, i: (l, i, 0, 0)),
                   pl.BlockSpec((1, 1, heads, LANES), lambda l, i: (l, i, 0, 0))),
        out_shape=(jax.ShapeDtypeStruct((depth, b, LANES, n), BF16), jax.ShapeDtypeStruct((depth, b, heads, LANES), F32)),
        compiler_params=_cparams(("parallel", "parallel")),
        name="cache_scan",
    )(lft)


def _select_head(q2, g, sub):
    lane = lax.broadcasted_iota(jnp.int32, (1, PAIR), 1)
    payload = (lane < LANES) & (lane // FOX_HEAD_DIM == sub)
    side = (lane >= LANES) & ((lane - LANES) // BIAS_SLOTS == 2 * g + sub)
    return jnp.where(payload | side, q2.astype(F32), 0.0).astype(BF16)


def _softmax_pv(s_c, s_n, vt_c, v_n):
    m = jnp.maximum(jnp.max(s_c, axis=1, keepdims=True), jnp.max(s_n, axis=1, keepdims=True))
    p_c = jnp.exp(s_c - m)
    p_n = jnp.exp(s_n - m)
    den = jnp.sum(p_c, axis=1, keepdims=True) + jnp.sum(p_n, axis=1, keepdims=True)
    return (_dot_nt(p_c.astype(BF16), vt_c) + _dot(p_n.astype(BF16), v_n)) / den


def _sample_attn_kernel(kct_ref, vct_ref, kbct_ref, ckvc_ref, krct_ref, wukvt_ref,
                        qf2_ref, kf2_ref, vf2_ref, qm2_ref, km2_ref, vm2_ref,
                        fox_ref, mla_ref, *, past, chunk):
    n = qf2_ref.shape[1]
    row = lax.broadcasted_iota(jnp.int32, (n, n), 0)
    col = lax.broadcasted_iota(jnp.int32, (n, n), 1)
    chunk_ok = ((past + col) // chunk) <= ((past + row) // chunk)
    row2 = lax.broadcasted_iota(jnp.int32, (2 * n, n), 0) % n
    frame_ok2 = lax.broadcasted_iota(jnp.int32, (2 * n, n), 1) <= row2
    first = lax.broadcasted_iota(jnp.int32, (1, LANES), 1) < FOX_HEAD_DIM
    kbct = kbct_ref[0, 0]

    for p in range(FOX_HEADS // 2):
        kkt = jnp.concatenate([kct_ref[0, 0, p * LANES:(p + 1) * LANES].astype(BF16), kbct], axis=0)
        vvt = vct_ref[0, 0, p * LANES:(p + 1) * LANES].astype(BF16)
        qpair = qf2_ref[0, :, p * PAIR:(p + 1) * PAIR]
        q_st = jnp.concatenate([_select_head(qpair, p, sub) for sub in range(2)], axis=0)
        s_c = _dot(q_st, kkt)
        s_n = _dot_nt(q_st, kf2_ref[0, :, p * PAIR:(p + 1) * PAIR])
        s_n = jnp.where(frame_ok2, s_n, -1e30)
        o = _softmax_pv(s_c, s_n, vvt, vf2_ref[0, :, p * PAIR:p * PAIR + LANES])
        fox_ref[0, :, p * LANES:(p + 1) * LANES] = jnp.where(first, o[:n], o[n:]).astype(fox_ref.dtype)

    kvt_c = _dot_nt(wukvt_ref[...], ckvc_ref[0, 0].astype(BF16))
    krct = krct_ref[0, 0].astype(BF16)
    pad = jnp.zeros((PAIR - MLA_NOPE_DIM - MLA_ROPE_DIM, krct.shape[1]), BF16)
    for hd in range(MLA_HEADS):
        kkt = jnp.concatenate([kvt_c[hd * LANES:(hd + 1) * LANES].astype(BF16), krct, pad], axis=0)
        vvt = kvt_c[MLA_HEADS * LANES + hd * LANES:MLA_HEADS * LANES + (hd + 1) * LANES].astype(BF16)
        q = qm2_ref[0, :, hd * PAIR:(hd + 1) * PAIR]
        s_c = _dot(q, kkt)
        s_n = jnp.where(chunk_ok, _dot_nt(q, km2_ref[0, :, hd * PAIR:(hd + 1) * PAIR]), -1e30)
        o = _softmax_pv(s_c, s_n, vvt, vm2_ref[0, :, hd * PAIR:hd * PAIR + LANES])
        mla_ref[0, :, hd * LANES:(hd + 1) * LANES] = o.astype(mla_ref.dtype)


def _sample_attention(l, kct, vct, kbct, ckvc, krct, wukvt, qf2, kf2, vf2, qm2, km2, vm2, *, chunk):
    b, n = qf2.shape[:2]
    past = kct.shape[3]
    cache = lambda a: pl.BlockSpec((1, 1) + a.shape[2:], lambda i: (l, i, 0, 0))
    per_b = lambda a: pl.BlockSpec((1,) + a.shape[1:], lambda i: (i, 0, 0))
    return pl.pallas_call(
        functools.partial(_sample_attn_kernel, past=past, chunk=chunk),
        grid=(b,),
        in_specs=[cache(kct), cache(vct), cache(kbct), cache(ckvc), cache(krct),
                  pl.BlockSpec(wukvt.shape, lambda i: (0, 0)),
                  per_b(qf2), per_b(kf2), per_b(vf2), per_b(qm2), per_b(km2), per_b(vm2)],
        out_specs=(pl.BlockSpec((1, n, FOX_WIDTH), lambda i: (i, 0, 0)),
                   pl.BlockSpec((1, n, MLA_WIDTH), lambda i: (i, 0, 0))),
        out_shape=(jax.ShapeDtypeStruct((b, n, FOX_WIDTH), BF16), jax.ShapeDtypeStruct((b, n, MLA_WIDTH), BF16)),
        compiler_params=_cparams(("parallel",)),
        name="sample_attn",
    )(kct, vct, kbct, ckvc, krct, wukvt, qf2, kf2, vf2, qm2, km2, vm2)


def _memkv_kernel(mem_ref, g_ref, wk_ref, wv_ref, mk_ref, mv_ref, mkb_ref, mvb_ref):
    m = _rms(mem_ref[0], g_ref[0]).astype(BF16)
    mk = _dot(m, wk_ref[0])
    mv = _dot(m, wv_ref[0])
    mk_ref[0, 0] = mk
    mv_ref[0, 0] = mv
    mkb_ref[0, 0] = mk.astype(BF16)
    mvb_ref[0, 0] = mv.astype(BF16)


def _memory_kv(mem, g, wk, wv):
    b, n, d = mem.shape
    depth = g.shape[0]
    o_spec = pl.BlockSpec((1, 1, n, d), lambda l, i: (l, i, 0, 0))
    w_spec = pl.BlockSpec((1, d, d), lambda l, i: (l, 0, 0))
    return pl.pallas_call(
        _memkv_kernel,
        grid=(depth, b),
        in_specs=[pl.BlockSpec((1, n, d), lambda l, i: (i, 0, 0)),
                  pl.BlockSpec((1, 1, d), lambda l, i: (l, 0, 0)), w_spec, w_spec],
        out_specs=(o_spec,) * 4,
        out_shape=(jax.ShapeDtypeStruct((depth, b, n, d), F32),) * 2 + (jax.ShapeDtypeStruct((depth, b, n, d), BF16),) * 2,
        compiler_params=_cparams(("arbitrary", "arbitrary")),
        name="memory_kv",
    )(mem, g, wk, wv)


def _post_kernel(x_ref, fox_ref, mla_ref, wo_ref, g_ref, wxq_ref, mk_ref, mv_ref, wxo_ref, o_ref):
    x = x_ref[0] + _dot(fox_ref[0], wo_ref[:FOX_WIDTH, :]) + _dot(mla_ref[0], wo_ref[FOX_WIDTH:, :])
    hc = _rms(x, g_ref[...]).astype(BF16)
    q = (_dot(hc, wxq_ref[...]) * (X_HEAD_DIM ** -0.5)).astype(BF16)
    by_head = len(mk_ref.shape) == 5
    outs = []
    for hd in range(X_HEADS):
        sl = slice(hd * X_HEAD_DIM, (hd + 1) * X_HEAD_DIM)
        mk = mk_ref[0, 0, :, hd, :] if by_head else mk_ref[0, 0, :, sl]
        mv = mv_ref[0, 0, :, hd, :] if by_head else mv_ref[0, 0, :, sl]
        s = _dot_nt(q[:, sl], mk.astype(BF16))
        p = jnp.exp(s - jnp.max(s, axis=1, keepdims=True))
        den = jnp.sum(p, axis=1, keepdims=True)
        outs.append((_dot(p.astype(BF16), mv.astype(BF16)) / den).astype(BF16))
    o_ref[0] = x + _dot(jnp.concatenate(outs, axis=1), wxo_ref[...])


def _post(l, x, fox, mla, wo, g, wxq, mk, mv, wxo, *, tm):
    b, s, d = x.shape
    tok = lambda w: pl.BlockSpec((1, tm, w), lambda i, j: (i, j, 0))
    full = lambda a: pl.BlockSpec(a.shape, lambda i, j: (0,) * a.ndim)
    mem = pl.BlockSpec((1, 1) + mk.shape[2:], lambda i, j: (l, i) + (0,) * (mk.ndim - 2))
    return pl.pallas_call(
        _post_kernel,
        grid=(b, s // tm),
        in_specs=[tok(d), tok(FOX_WIDTH), tok(MLA_WIDTH), full(wo), full(g), full(wxq), mem, mem, full(wxo)],
        out_specs=tok(d),
        out_shape=jax.ShapeDtypeStruct((b, s, d), F32),
        compiler_params=_cparams(("parallel", "arbitrary")),
        name="post",
    )(x, fox, mla, wo, g, wxq, mk, mv, wxo)


def _mlp_kernel(x_ref, g_ref, wup_ref, wdn_ref, gf_ref, o_ref, *, final, ff_blk):
    x = x_ref[...]
    hm = _rms(x, g_ref[...]).astype(BF16)
    acc = x
    for c in range(D_FF // ff_blk):
        u = jnp.maximum(_dot(hm, wup_ref[:, c * ff_blk:(c + 1) * ff_blk]), 0.0)
        acc = acc + _dot(jnp.square(u).astype(BF16), wdn_ref[c * ff_blk:(c + 1) * ff_blk, :])
    o_ref[...] = _rms(acc, gf_ref[...]) if final else acc


def _mlp(x, g, wup, wdn, gf, *, final, tm):
    n, d = x.shape
    full = lambda a: pl.BlockSpec(a.shape, lambda i: (0,) * a.ndim, pipeline_mode=pl.Buffered(1))
    return pl.pallas_call(
        functools.partial(_mlp_kernel, final=final, ff_blk=1024),
        grid=(n // tm,),
        in_specs=[pl.BlockSpec((tm, d), lambda i: (i, 0)), full(g), full(wup), full(wdn), full(gf)],
        out_specs=pl.BlockSpec((tm, d), lambda i: (i, 0)),
        out_shape=jax.ShapeDtypeStruct((n, d), F32),
        compiler_params=_cparams(("parallel",)),
        name="mlp",
    )(x, g, wup, wdn, gf)


def _gate_block(a):
    pad = jnp.zeros(a.shape[:-1] + (LANES - FOX_HEADS * BIAS_SLOTS - FOX_HEADS,), a.dtype)
    return jnp.concatenate([jnp.repeat(a, BIAS_SLOTS, axis=-1), a, pad], axis=-1)


def _split_win(w):
    o = 3 * FOX_WIDTH
    o2 = o + FOX_HEADS + MLA_Q_RANK
    half = MLA_ROPE_DIM // 2
    kr = w[:, o2 + MLA_KV_RANK:]
    return dict(q=w[:, :FOX_WIDTH], k=w[:, FOX_WIDTH:2 * FOX_WIDTH], v=w[:, 2 * FOX_WIDTH:o], g=w[:, o:o + FOX_HEADS],
                cq=w[:, o + FOX_HEADS:o2], ckv=w[:, o2:o2 + MLA_KV_RANK], kr=kr,
                kr_sw=jnp.concatenate([kr[:, half:], kr[:, :half]], axis=1))


def _layout_win_sample(w):
    p = _split_win(w)
    return jnp.concatenate([p["q"], p["k"], p["v"], p["cq"], p["ckv"], p["kr"], p["kr_sw"], _gate_block(p["g"])], axis=1).astype(BF16)


def _layout_win_prompt(w):
    p = _split_win(w)
    wr = jnp.concatenate([p["k"], p["ckv"], p["kr"], p["kr_sw"], _gate_block(p["g"])], axis=1).astype(BF16)
    wt = jnp.concatenate([p["q"], p["k"], p["v"], p["cq"], p["ckv"], p["kr"]], axis=1).T.astype(BF16)
    return wr, wt


def _layout_wuq_sample(w):
    w = w.reshape(MLA_Q_RANK, MLA_HEADS, MLA_NOPE_DIM + MLA_ROPE_DIM)
    nope = w[:, :, :MLA_NOPE_DIM].reshape(MLA_Q_RANK, MLA_HEADS * MLA_NOPE_DIM)
    r = w[:, :, MLA_NOPE_DIM:]
    half = MLA_ROPE_DIM // 2
    r2 = jnp.concatenate([r, r[:, :, half:], r[:, :, :half]], axis=2).reshape(MLA_Q_RANK, MLA_HEADS * LANES)
    return jnp.concatenate([nope, r2], axis=1).astype(BF16)


def _layout_wuq_prompt(w):
    w = w.reshape(MLA_Q_RANK, MLA_HEADS, MLA_NOPE_DIM + MLA_ROPE_DIM)
    return jnp.concatenate([w[:, :, :MLA_NOPE_DIM].reshape(MLA_Q_RANK, -1),
                            w[:, :, MLA_NOPE_DIM:].reshape(MLA_Q_RANK, -1)], axis=1).T.astype(BF16)


def _layout_wukv(w):
    w = w.reshape(MLA_KV_RANK, MLA_HEADS, MLA_NOPE_DIM + MLA_V_DIM)
    return jnp.concatenate([w[:, :, :MLA_NOPE_DIM].reshape(MLA_KV_RANK, -1),
                            w[:, :, MLA_NOPE_DIM:].reshape(MLA_KV_RANK, -1)], axis=1).astype(BF16)


def _rope_tables(pos):
    half = MLA_ROPE_DIM // 2
    inv = ROPE_THETA ** (-jnp.arange(half, dtype=F32) / half)
    ang = pos.astype(F32)[:, None] * inv[None, :]
    return jnp.cos(ang), jnp.sin(ang)


def _tile(n, pref):
    return pref if n % pref == 0 else n


def kernel(x_prompt, x_sample, mem_prompt, cache_fox_k, cache_fox_v, cache_fox_logf, cache_mla_ckv, cache_mla_krope,
           cache_mem_k, cache_mem_v, norm_mix, w_in, b_forget, mla_q_norm, w_uq, mla_kv_norm, w_ukv, w_out,
           norm_cross, norm_mem, w_xq, w_mk, w_mv, w_xo, norm_mlp, w_up, w_down, norm_final):
    depth = w_in.shape[0]
    bp, sp, d = x_prompt.shape
    bs, ss, _ = x_sample.shape
    past = cache_fox_k.shape[2]
    n_mem = mem_prompt.shape[1]
    row = lambda a: a.reshape(1, -1)
    col = lambda a: a.reshape(-1, 1)

    cos_p, sin_p = _rope_tables(jnp.arange(sp))
    cos_s, sin_s = _rope_tables(past + jnp.arange(ss))
    cs_p = jnp.concatenate([cos_p, cos_p, -sin_p, sin_p], axis=1)
    cs_s = jnp.concatenate([cos_s, cos_s, -sin_s, sin_s], axis=1)
    cos_pt, sin_pt = cos_p.T, sin_p.T

    mk_all, mv_all, mkb_all, mvb_all = _memory_kv(mem_prompt, norm_mem[:, None, :], w_mk.astype(BF16), w_mv.astype(BF16))

    kct = jnp.transpose(cache_fox_k, (0, 1, 3, 4, 2)).reshape(depth, bs, FOX_WIDTH, past)
    vct = jnp.transpose(cache_fox_v, (0, 1, 3, 4, 2)).reshape(depth, bs, FOX_WIDTH, past)
    krct = jnp.transpose(cache_mla_krope, (0, 1, 3, 2))
    kbct, tot = _cache_scan(jnp.transpose(cache_fox_logf, (0, 1, 3, 2)).astype(F32))
    cin_all = _gate_block(tot[..., 0])[:, :, None, :]

    tm_p = _tile(sp, 512)
    xp, xs = x_prompt, x_sample
    kt_all = vt_all = None
    rows_p, rows_s = [], []
    for l in range(depth):
        wsplit_s = _layout_win_sample(w_in[l])
        wr, wt = _layout_win_prompt(w_in[l])
        bblk = _gate_block(b_forget[l][None, :])
        wukv = _layout_wukv(w_ukv[l])
        wnope, wvt = wukv[:, :MLA_HEADS * MLA_NOPE_DIM], wukv[:, MLA_HEADS * MLA_NOPE_DIM:].T
        wo, wxq, wxo = w_out[l].astype(BF16), w_xq[l].astype(BF16), w_xo[l].astype(BF16)
        wup, wdn = w_up[l].astype(BF16), w_down[l].astype(BF16)
        gmix, gq, gkv, gx, gm = row(norm_mix[l]), row(mla_q_norm[l]), row(mla_kv_norm[l]), row(norm_cross[l]), row(norm_mlp[l])
        final = l == depth - 1

        kt_all, vt_all, lft, ckv, krt, qt, k2, vt2, qmt, km2, vmt = _inproj_prompt(
            xp, cs_p, cos_pt, sin_pt, gmix, wr, wt, bblk, col(mla_q_norm[l]), _layout_wuq_prompt(w_uq[l]),
            gkv, col(mla_kv_norm[l]), wnope, wvt, kt_all, vt_all, tm=tm_p)
        fox = _attention(qt, k2, vt2, n_chain=2, n_sub=2, chunk=1, v_rows=FOX_V_ROWS, v_dim=FOX_HEAD_DIM)
        mla = _attention(qmt, km2, vmt, n_chain=2, n_sub=1, chunk=CHUNK, v_rows=MLA_V_ROWS, v_dim=MLA_V_DIM)
        xp = _post(l, xp, fox, mla, wo, gx, wxq, mkb_all, mvb_all, wxo, tm=tm_p)
        xp = _mlp(xp.reshape(bp * sp, d), gm, wup, wdn, row(norm_final), final=final, tm=_tile(bp * sp, 512)).reshape(bp, sp, d)
        rows_p.append((lft, ckv, krt))

        kf, vf, lf, ckv, kr, qf2, kf2, vf2, qm2, km2, vm2 = _inproj(
            xs, cin_all[l], cs_s, gmix, wsplit_s, bblk, gq, _layout_wuq_sample(w_uq[l]), gkv, wukv, tm=ss)
        fox, mla = _sample_attention(l, kct, vct, kbct, cache_mla_ckv, krct, wukv.T, qf2, kf2, vf2, qm2, km2, vm2, chunk=CHUNK)
        xs = _post(l, xs, fox, mla, wo, gx, wxq, cache_mem_k, cache_mem_v, wxo, tm=ss)
        xs = _mlp(xs.reshape(bs * ss, d), gm, wup, wdn, row(norm_final), final=final, tm=_tile(bs * ss, 512)).reshape(bs, ss, d)
        rows_s.append((kf, vf, lf, ckv, kr))

    st = lambda rows, i: jnp.stack([r[i] for r in rows])
    heads5 = lambda a: jnp.transpose(a.reshape(depth, bp, FOX_HEADS, FOX_HEAD_DIM, sp), (0, 1, 4, 2, 3))
    lo = FOX_HEADS * BIAS_SLOTS
    mem_shape = (depth, bp, n_mem, X_HEADS, X_HEAD_DIM)
    return (xp, xs,
            heads5(kt_all), heads5(vt_all), jnp.transpose(st(rows_p, 0), (0, 1, 3, 2)), st(rows_p, 1),
            jnp.transpose(st(rows_p, 2), (0, 1, 3, 2)),
            mk_all.reshape(mem_shape), mv_all.reshape(mem_shape),
            st(rows_s, 0).reshape(depth, bs, ss, FOX_HEADS, FOX_HEAD_DIM), st(rows_s, 1).reshape(depth, bs, ss, FOX_HEADS, FOX_HEAD_DIM),
            st(rows_s, 2)[..., lo:lo + FOX_HEADS], st(rows_s, 3), st(rows_s, 4))
```

```python
import functools

import jax
import jax.numpy as jnp
from jax import lax
from jax.experimental import pallas as pl
from jax.experimental.pallas import tpu as pltpu

F32 = jnp.float32
BF16 = jnp.bfloat16

D_MODEL = 1024
CHUNK = 64
FOX_HEADS = 8
FOX_HEAD_DIM = 64
FOX_WIDTH = FOX_HEADS * FOX_HEAD_DIM
MLA_HEADS = 4
MLA_NOPE_DIM = 128
MLA_ROPE_DIM = 64
MLA_V_DIM = 128
MLA_WIDTH = MLA_HEADS * MLA_V_DIM
MLA_Q_RANK = 256
MLA_KV_RANK = 128
ROPE_THETA = 10000.0
X_HEADS = 4
X_HEAD_DIM = 256
D_FF = 4 * D_MODEL
EPS = 1e-6

LOG2E = 1.4426950408889634
LANES = 128
BF16_ROWS = 16
PAIR = 2 * LANES
BIAS_SLOTS = 8
VMEM_LIMIT = 56 * 1024 * 1024
FOX_V_ROWS = FOX_HEAD_DIM + BF16_ROWS
MLA_V_ROWS = MLA_V_DIM + BF16_ROWS

_OQ, _OK, _OV = 0, FOX_WIDTH, 2 * FOX_WIDTH
_OCQ = 3 * FOX_WIDTH
_OCKV = _OCQ + MLA_Q_RANK
_OKR = _OCKV + MLA_KV_RANK
_OG = _OKR + LANES
_RK, _RCKV, _RKR, _RG = 0, FOX_WIDTH, FOX_WIDTH + MLA_KV_RANK, FOX_WIDTH + MLA_KV_RANK + LANES
_TQ, _TV = 0, FOX_WIDTH
_TCQ = 2 * FOX_WIDTH
_TCKV = _TCQ + MLA_Q_RANK
_TKR = _TCKV + MLA_KV_RANK


def _cparams(sem):
    return pltpu.CompilerParams(dimension_semantics=sem, vmem_limit_bytes=VMEM_LIMIT)


def _rms(x, g):
    return x * lax.rsqrt(jnp.mean(jnp.square(x), axis=-1, keepdims=True) + EPS) * g


def _rms_rows(x, g):
    return x * lax.rsqrt(jnp.mean(jnp.square(x), axis=0, keepdims=True) + EPS) * g


def _dot(a, b):
    return jnp.dot(a, b, preferred_element_type=F32)


def _dot_nt(a, b):
    return lax.dot_general(a, b, (((1,), (1,)), ((), ())), preferred_element_type=F32)


def _split3(x):
    x1 = x.astype(BF16)
    r1 = x - x1.astype(F32)
    x2 = r1.astype(BF16)
    r2 = r1 - x2.astype(F32)
    return x1, x2, r2.astype(BF16)


def _cumsum_rows(x, carry):
    n = x.shape[0]
    blk = min(n, 256)
    r = lax.broadcasted_iota(jnp.int32, (blk, blk), 0)
    c = lax.broadcasted_iota(jnp.int32, (blk, blk), 1)
    tri = (c <= r).astype(BF16)
    outs = []
    for i in range(n // blk):
        x1, x2, x3 = _split3(x[i * blk:(i + 1) * blk])
        cb = _dot(tri, x1) + _dot(tri, x2) + _dot(tri, x3) + carry
        carry = cb[blk - 1:blk]
        outs.append(cb)
    return outs[0] if len(outs) == 1 else jnp.concatenate(outs, axis=0)


def _cumsum_lanes(x):
    n = x.shape[1]
    blk = min(n, 256)
    r = lax.broadcasted_iota(jnp.int32, (blk, blk), 0)
    c = lax.broadcasted_iota(jnp.int32, (blk, blk), 1)
    tri = (r <= c).astype(BF16)
    carry = jnp.zeros((x.shape[0], 1), F32)
    outs = []
    for i in range(n // blk):
        x1, x2, x3 = _split3(x[:, i * blk:(i + 1) * blk])
        cb = _dot(x1, tri) + _dot(x2, tri) + _dot(x3, tri) + carry
        carry = cb[:, blk - 1:blk]
        outs.append(cb)
    return outs[0] if len(outs) == 1 else jnp.concatenate(outs, axis=1)


def _bias_blocks(cum, slot_axis):
    c1, c2, c3 = (c.astype(F32) for c in _split3(cum))
    slot = lax.broadcasted_iota(jnp.int32, cum.shape, slot_axis)
    j = slot % BIAS_SLOTS
    valid = slot < FOX_HEADS * BIAS_SLOTS
    one = jnp.where(valid & (j < 6), 1.0, 0.0)
    qb = jnp.where(j == 0, c1, jnp.where(j == 1, c2, jnp.where(j == 2, c3, one)))
    kb = jnp.where(j < 3, one, jnp.where(j == 3, -c1, jnp.where(j == 4, -c2, jnp.where(j == 5, -c3, 0.0))))
    return jnp.where(valid, qb, 0.0), jnp.where(valid, kb, 0.0)


def _log_sigmoid(x):
    return jnp.minimum(x, 0.0) - jnp.log1p(jnp.exp(-jnp.abs(x)))


def _rope_pair(t, cs):
    u = t * cs
    return u + pltpu.roll(u, MLA_ROPE_DIM, 1)


def _rope_rows(x, cos, sin):
    half = MLA_ROPE_DIM // 2
    x1, x2 = x[:half], x[half:]
    return jnp.concatenate([x1 * cos - x2 * sin, x2 * cos + x1 * sin], axis=0)


def _inproj_kernel(x_ref, cin_ref, cs_ref, gmix_ref, win_ref, bblk_ref, gq_ref, wuq_ref, gkv_ref, wukv_ref,
                   kf_ref, vf_ref, lf_ref, ckv_ref, kr_ref,
                   qf2_ref, kf2_ref, vf2_ref, qm2_ref, km2_ref, vm2_ref, *, seg):
    x = x_ref[0]
    tm = x.shape[0]
    h = _rms(x, gmix_ref[...])
    z = _dot(h.astype(BF16), win_ref[...])
    cs = cs_ref[...]
    lane = lax.broadcasted_iota(jnp.int32, (1, LANES), 1)
    low = lane < MLA_ROPE_DIM
    ones_blk = jnp.ones((tm, LANES), BF16)

    kf_ref[0] = z[:, _OK:_OK + FOX_WIDTH]
    vf_ref[0] = z[:, _OV:_OV + FOX_WIDTH]

    lf = _log_sigmoid(z[:, _OG:_OG + LANES] + bblk_ref[...])
    lf_ref[0] = lf
    cums = [_cumsum_rows(lf[i * seg:(i + 1) * seg], cin_ref[i]) for i in range(tm // seg)]
    cum = cums[0] if len(cums) == 1 else jnp.concatenate(cums, axis=0)
    qb, kb = _bias_blocks(cum, 1)

    fox_scale = FOX_HEAD_DIM ** -0.5
    for p in range(FOX_HEADS // 2):
        lo, hi = p * PAIR, p * PAIR + LANES
        qf2_ref[0, :, lo:hi] = (z[:, _OQ + p * LANES:_OQ + (p + 1) * LANES] * fox_scale).astype(BF16)
        qf2_ref[0, :, hi:hi + LANES] = qb.astype(BF16)
        kf2_ref[0, :, lo:hi] = z[:, _OK + p * LANES:_OK + (p + 1) * LANES].astype(BF16)
        kf2_ref[0, :, hi:hi + LANES] = kb.astype(BF16)
        vf2_ref[0, :, lo:hi] = z[:, _OV + p * LANES:_OV + (p + 1) * LANES].astype(BF16)
        vf2_ref[0, :, hi:hi + LANES] = ones_blk

    kr = _rope_pair(z[:, _OKR:_OKR + LANES], cs)
    kr_ref[0] = kr[:, :MLA_ROPE_DIM]
    krb = jnp.where(low, kr, 0.0).astype(BF16)

    ckv = _rms(z[:, _OCKV:_OCKV + MLA_KV_RANK], gkv_ref[...])
    ckv_ref[0] = ckv
    kv = _dot(ckv.astype(BF16), wukv_ref[...])
    cqn = _rms(z[:, _OCQ:_OCQ + MLA_Q_RANK], gq_ref[...])
    qm = _dot(cqn.astype(BF16), wuq_ref[...])
    mla_scale = (MLA_NOPE_DIM + MLA_ROPE_DIM) ** -0.5
    for hd in range(MLA_HEADS):
        lo, hi = hd * PAIR, hd * PAIR + LANES
        qr = _rope_pair(qm[:, MLA_HEADS * LANES + hd * LANES:MLA_HEADS * LANES + (hd + 1) * LANES], cs)
        qm2_ref[0, :, lo:hi] = (qm[:, hd * LANES:(hd + 1) * LANES] * mla_scale).astype(BF16)
        qm2_ref[0, :, hi:hi + LANES] = (jnp.where(low, qr, 0.0) * mla_scale).astype(BF16)
        km2_ref[0, :, lo:hi] = kv[:, hd * LANES:(hd + 1) * LANES].astype(BF16)
        km2_ref[0, :, hi:hi + LANES] = krb
        vm2_ref[0, :, lo:hi] = kv[:, MLA_HEADS * LANES + hd * LANES:MLA_HEADS * LANES + (hd + 1) * LANES].astype(BF16)
        vm2_ref[0, :, hi:hi + LANES] = ones_blk


def _inproj(x, cin, cs, gmix, win, bblk, gq, wuq, gkv, wukv, *, tm, seg):
    b, s, _ = x.shape
    grid = (b, s // tm)
    tok = lambda w: pl.BlockSpec((1, tm, w), lambda i, j: (i, j, 0))
    full = lambda a: pl.BlockSpec(a.shape, lambda i, j: (0,) * a.ndim)
    out_shape = (
        jax.ShapeDtypeStruct((b, s, FOX_WIDTH), F32), jax.ShapeDtypeStruct((b, s, FOX_WIDTH), F32),
        jax.ShapeDtypeStruct((b, s, LANES), F32), jax.ShapeDtypeStruct((b, s, MLA_KV_RANK), F32),
        jax.ShapeDtypeStruct((b, s, MLA_ROPE_DIM), F32),
    ) + tuple(jax.ShapeDtypeStruct((b, s, 4 * PAIR), BF16) for _ in range(6))
    out_specs = (tok(FOX_WIDTH), tok(FOX_WIDTH), tok(LANES), tok(MLA_KV_RANK), tok(MLA_ROPE_DIM)) + (tok(4 * PAIR),) * 6
    return pl.pallas_call(
        functools.partial(_inproj_kernel, seg=seg),
        grid=grid,
        in_specs=[tok(D_MODEL), pl.BlockSpec((tm // seg, 1, LANES), lambda i, j: (j, 0, 0)),
                  pl.BlockSpec((tm, LANES), lambda i, j: (j, 0)),
                  full(gmix), full(win), full(bblk), full(gq), full(wuq), full(gkv), full(wukv)],
        out_specs=out_specs,
        out_shape=out_shape,
        compiler_params=_cparams(("parallel", "parallel")),
        name="inproj_sample",
    )(x, cin, cs, gmix, win, bblk, gq, wuq, gkv, wukv)


def _inproj_prompt_kernel(*refs, n_prev):
    (x_ref, cs_ref, cos_ref, sin_ref, gmix_ref, wr_ref, wt_ref, bblk_ref, gq_ref, wuqt_ref, gkv_ref, gkvc_ref,
     wnope_ref, wvt_ref) = refs[:14]
    prev_k_ref, prev_v_ref = (refs[14], refs[15]) if n_prev else (None, None)
    (kt_ref, vt_ref, lft_ref, ckv_ref, krt_ref,
     qt_ref, k2_ref, vt2_ref, qmt_ref, km2_ref, vmt_ref, carry_ref) = refs[14 + (2 if n_prev else 0):]

    @pl.when(pl.program_id(1) == 0)
    def _():
        carry_ref[...] = jnp.zeros_like(carry_ref)

    x = x_ref[0]
    tm = x.shape[0]
    hb = _rms(x, gmix_ref[...]).astype(BF16)
    zr = _dot(hb, wr_ref[...])
    zt = _dot_nt(wt_ref[...], hb)
    cs, cos, sin = cs_ref[...], cos_ref[...], sin_ref[...]
    ones_rows = jnp.ones((BF16_ROWS, tm), BF16)

    if n_prev:
        kt_ref[:n_prev, 0] = prev_k_ref[:, 0]
        vt_ref[:n_prev, 0] = prev_v_ref[:, 0]
    kt_ref[n_prev, 0] = zr[:, _RK:_RK + FOX_WIDTH].T
    vt_ref[n_prev, 0] = zt[_TV:_TV + FOX_WIDTH]

    lf = _log_sigmoid(zr[:, _RG:_RG + LANES] + bblk_ref[...])
    cum = _cumsum_rows(lf, carry_ref[...])
    carry_ref[...] = cum[tm - 1:tm]
    qb, kb = _bias_blocks(cum * LOG2E, 1)
    kb = kb.astype(BF16)
    qbt = qb.T
    lo = FOX_HEADS * BIAS_SLOTS
    lft_ref[0] = lf.T[lo:lo + FOX_HEADS]

    fox_scale = FOX_HEAD_DIM ** -0.5 * LOG2E
    rowi = lax.broadcasted_iota(jnp.int32, (LANES, tm), 0)
    for p in range(FOX_HEADS // 2):
        zq = zt[_TQ + p * LANES:_TQ + (p + 1) * LANES] * fox_scale
        for sub in range(2):
            hd = 2 * p + sub
            qt_ref[0, 0, hd * PAIR:hd * PAIR + LANES] = jnp.where(rowi // FOX_HEAD_DIM == sub, zq, 0.0).astype(BF16)
            qt_ref[0, 0, hd * PAIR + LANES:(hd + 1) * PAIR] = jnp.where(rowi // BIAS_SLOTS == hd, qbt, 0.0).astype(BF16)
        k2_ref[0, :, p * PAIR:p * PAIR + LANES] = zr[:, _RK + p * LANES:_RK + (p + 1) * LANES].astype(BF16)
        k2_ref[0, :, p * PAIR + LANES:(p + 1) * PAIR] = kb
    for hd in range(FOX_HEADS):
        r0 = hd * FOX_V_ROWS
        vt2_ref[0, 0, r0:r0 + FOX_HEAD_DIM] = zt[_TV + hd * FOX_HEAD_DIM:_TV + (hd + 1) * FOX_HEAD_DIM].astype(BF16)
        vt2_ref[0, 0, r0 + FOX_HEAD_DIM:r0 + FOX_V_ROWS] = ones_rows

    krt_ref[0] = _rope_rows(zt[_TKR:_TKR + MLA_ROPE_DIM], cos, sin)
    kr = _rope_pair(zr[:, _RKR:_RKR + LANES], cs)
    krb = jnp.where(lax.broadcasted_iota(jnp.int32, (1, LANES), 1) < MLA_ROPE_DIM, kr, 0.0).astype(BF16)
    ckv = _rms(zr[:, _RCKV:_RCKV + MLA_KV_RANK], gkv_ref[...])
    ckv_ref[0] = ckv
    knope = _dot(ckv.astype(BF16), wnope_ref[...])
    ckvt = _rms_rows(zt[_TCKV:_TCKV + MLA_KV_RANK], gkvc_ref[...])
    vmt = _dot(wvt_ref[...], ckvt.astype(BF16))
    cqnt = _rms_rows(zt[_TCQ:_TCQ + MLA_Q_RANK], gq_ref[...])
    qmt = _dot(wuqt_ref[...], cqnt.astype(BF16))
    mla_scale = (MLA_NOPE_DIM + MLA_ROPE_DIM) ** -0.5 * LOG2E
    zeros_rows = jnp.zeros((PAIR - MLA_NOPE_DIM - MLA_ROPE_DIM, tm), BF16)
    for hd in range(MLA_HEADS):
        q0 = hd * PAIR
        qmt_ref[0, 0, q0:q0 + MLA_NOPE_DIM] = (qmt[hd * MLA_NOPE_DIM:(hd + 1) * MLA_NOPE_DIM] * mla_scale).astype(BF16)
        r0 = MLA_HEADS * MLA_NOPE_DIM + hd * MLA_ROPE_DIM
        qmt_ref[0, 0, q0 + MLA_NOPE_DIM:q0 + MLA_NOPE_DIM + MLA_ROPE_DIM] = (
            _rope_rows(qmt[r0:r0 + MLA_ROPE_DIM], cos, sin) * mla_scale).astype(BF16)
        qmt_ref[0, 0, q0 + MLA_NOPE_DIM + MLA_ROPE_DIM:q0 + PAIR] = zeros_rows
        km2_ref[0, :, q0:q0 + LANES] = knope[:, hd * LANES:(hd + 1) * LANES].astype(BF16)
        km2_ref[0, :, q0 + LANES:q0 + PAIR] = krb
        v0 = hd * MLA_V_ROWS
        vmt_ref[0, 0, v0:v0 + MLA_V_DIM] = vmt[hd * MLA_V_DIM:(hd + 1) * MLA_V_DIM].astype(BF16)
        vmt_ref[0, 0, v0 + MLA_V_DIM:v0 + MLA_V_ROWS] = ones_rows


def _inproj_prompt(x, cs, cos_t, sin_t, gmix, wr, wt, bblk, gq_col, wuqt, gkv, gkv_col, wnope, wvt, prev_k, prev_v, *, tm):
    b, s, _ = x.shape
    nt = s // tm
    n_prev = 0 if prev_k is None else prev_k.shape[0]
    full = lambda a: pl.BlockSpec(a.shape, lambda i, j: (0,) * a.ndim, pipeline_mode=pl.Buffered(1))
    tile4 = lambda rows: pl.BlockSpec((1, 1, rows, tm), lambda i, j: (i, j, 0, 0))
    tok = lambda w: pl.BlockSpec((1, tm, w), lambda i, j: (i, j, 0))
    feat = lambda rows: pl.BlockSpec((1, rows, tm), lambda i, j: (i, 0, j))
    stacked = lambda n: pl.BlockSpec((n, 1, FOX_WIDTH, tm), lambda i, j: (0, i, 0, j))
    ins = [x, cs, cos_t, sin_t, gmix, wr, wt, bblk, gq_col, wuqt, gkv, gkv_col, wnope, wvt]
    in_specs = [tok(D_MODEL), pl.BlockSpec((tm, LANES), lambda i, j: (j, 0)),
                pl.BlockSpec((cos_t.shape[0], tm), lambda i, j: (0, j)),
                pl.BlockSpec((sin_t.shape[0], tm), lambda i, j: (0, j))] + [full(a) for a in ins[4:]]
    if n_prev:
        ins += [prev_k, prev_v]
        in_specs += [stacked(n_prev), stacked(n_prev)]
    op4 = lambda rows: jax.ShapeDtypeStruct((b, nt, rows, tm), BF16)
    out_shape = (jax.ShapeDtypeStruct((n_prev + 1, b, FOX_WIDTH, s), F32),) * 2 + (
        jax.ShapeDtypeStruct((b, FOX_HEADS, s), F32), jax.ShapeDtypeStruct((b, s, MLA_KV_RANK), F32),
        jax.ShapeDtypeStruct((b, MLA_ROPE_DIM, s), F32),
        op4(FOX_HEADS * PAIR), jax.ShapeDtypeStruct((b, s, 4 * PAIR), BF16), op4(FOX_HEADS * FOX_V_ROWS),
        op4(MLA_HEADS * PAIR), jax.ShapeDtypeStruct((b, s, 4 * PAIR), BF16), op4(MLA_HEADS * MLA_V_ROWS))
    out_specs = (stacked(n_prev + 1),) * 2 + (
        feat(FOX_HEADS), tok(MLA_KV_RANK), feat(MLA_ROPE_DIM),
        tile4(FOX_HEADS * PAIR), tok(4 * PAIR), tile4(FOX_HEADS * FOX_V_ROWS),
        tile4(MLA_HEADS * PAIR), tok(4 * PAIR), tile4(MLA_HEADS * MLA_V_ROWS))
    return pl.pallas_call(
        functools.partial(_inproj_prompt_kernel, n_prev=n_prev),
        grid=(b, nt),
        in_specs=in_specs,
        out_specs=out_specs,
        out_shape=out_shape,
        scratch_shapes=[pltpu.VMEM((1, LANES), F32)],
        compiler_params=_cparams(("parallel", "arbitrary")),
        name="inproj_prompt",
    )(*ins)


def _attn_kernel(qt_ref, k_ref, vt_ref, o_ref, sa_ref, sb_ref, sd_ref, m_ref, acc_ref, *,
                 t, nt, n_chain, n_sub, chunk, v_rows, v_dim):
    i = pl.program_id(2)
    key = lax.broadcasted_iota(jnp.int32, (t, t), 0)
    qry = lax.broadcasted_iota(jnp.int32, (t, t), 1)
    allowed = (key // chunk) <= (qry // chunk)

    def scores(qblk, kblk, dst_ref, diagonal=False):
        off = pl.multiple_of(kblk * t, t)
        for c in range(n_chain):
            g = c // n_sub
            st = _dot(k_ref[0, pl.ds(off, t), g * PAIR:(g + 1) * PAIR], qt_ref[0, qblk, c * PAIR:(c + 1) * PAIR])
            dst_ref[c] = jnp.where(allowed, st, -1e30) if diagonal else st

    def consume(kblk, src_ref):
        for c in range(n_chain):
            st = src_ref[c]
            m = m_ref[c]
            m_new = jnp.maximum(m, jnp.max(st, axis=0, keepdims=True))
            pt = jnp.exp2(st - m_new).astype(BF16)
            acc_ref[c] = jnp.exp2(m - m_new) * acc_ref[c] + _dot(vt_ref[0, kblk, c * v_rows:(c + 1) * v_rows], pt)
            m_ref[c] = m_new

    def next_diagonal():
        nxt = jnp.minimum(i + 1, nt - 1)
        scores(nxt, nxt, sd_ref, diagonal=True)

    @pl.when(i == 0)
    def _():
        scores(0, 0, sd_ref, diagonal=True)

    m_ref[...] = jnp.full(m_ref.shape, -1e30, F32)
    acc_ref[...] = jnp.zeros(acc_ref.shape, F32)

    @pl.when(i == 0)
    def _():
        consume(0, sd_ref)
        next_diagonal()

    @pl.when(i > 0)
    def _():
        scores(i, 0, sa_ref)
        consume(i, sd_ref)
        n_pairs = (i - 1) // 2

        def pair(p, carry):
            scores(i, 2 * p + 1, sb_ref)
            consume(2 * p, sa_ref)
            scores(i, 2 * p + 2, sa_ref)
            consume(2 * p + 1, sb_ref)
            return carry

        lax.fori_loop(0, n_pairs, pair, 0)
        one_more = (i - 1) % 2 == 1

        @pl.when(one_more)
        def _():
            scores(i, i - 1, sb_ref)
            consume(2 * n_pairs, sa_ref)
            next_diagonal()
            consume(i - 1, sb_ref)

        @pl.when(jnp.logical_not(one_more))
        def _():
            next_diagonal()
            consume(2 * n_pairs, sa_ref)

    for g in range(n_chain // n_sub):
        outs = []
        for sub in range(n_sub):
            acc = acc_ref[g * n_sub + sub]
            outs.append(acc[:v_dim] / acc[v_dim:v_dim + 1])
        ot = outs[0] if n_sub == 1 else jnp.concatenate(outs, axis=0)
        o_ref[0, :, g * LANES:(g + 1) * LANES] = ot.T.astype(o_ref.dtype)


def _attention(qt, k2, vt, *, n_chain, n_sub, chunk, v_rows, v_dim):
    b, nt, rows, t = qt.shape
    s = nt * t
    steps = rows // (n_chain * PAIR)
    gps = n_chain // n_sub
    return pl.pallas_call(
        functools.partial(_attn_kernel, t=t, nt=nt, n_chain=n_chain, n_sub=n_sub, chunk=chunk, v_rows=v_rows, v_dim=v_dim),
        grid=(b, steps, nt),
        in_specs=[pl.BlockSpec((1, nt, n_chain * PAIR, t), lambda bi, g, i: (bi, 0, g, 0)),
                  pl.BlockSpec((1, s, gps * PAIR), lambda bi, g, i: (bi, 0, g)),
                  pl.BlockSpec((1, nt, n_chain * v_rows, t), lambda bi, g, i: (bi, 0, g, 0))],
        out_specs=pl.BlockSpec((1, t, gps * LANES), lambda bi, g, i: (bi, i, g)),
        out_shape=jax.ShapeDtypeStruct((b, s, 4 * LANES), BF16),
        scratch_shapes=[pltpu.VMEM((n_chain, t, t), F32)] * 3
                       + [pltpu.VMEM((n_chain, 1, t), F32), pltpu.VMEM((n_chain, v_rows, t), F32)],
        compiler_params=_cparams(("parallel", "arbitrary", "arbitrary")),
        name="attn_fox" if n_sub > 1 else "attn_mla",
    )(qt, k2, vt)


def _cache_scan_kernel(lft_ref, kbct_ref, tot_ref):
    lf = lft_ref[0, 0]
    heads, n = lf.shape
    cum = _cumsum_lanes(jnp.concatenate([lf, jnp.zeros_like(lf)], axis=0))[:heads]
    tot_ref[0, 0] = jnp.broadcast_to(cum[:, n - 1:n], (heads, LANES))
    rep = jnp.concatenate([jnp.broadcast_to(cum[hd:hd + 1], (BIAS_SLOTS, n)) for hd in range(heads)]
                          + [jnp.zeros((LANES - heads * BIAS_SLOTS, n), F32)], axis=0)
    kbct_ref[0, 0] = _bias_blocks(rep, 0)[1].astype(BF16)


def _cache_scan(lft):
    depth, b, heads, n = lft.shape
    return pl.pallas_call(
        _cache_scan_kernel,
        grid=(depth, b),
        in_specs=[pl.BlockSpec((1, 1, heads, n), lambda l, i: (l, i, 0, 0))],
        out_specs=(pl.BlockSpec((1, 1, LANES, n), lambda l, i: (l, i, 0, 0)),
                   pl.BlockSpec((1, 1, heads, LANES), lambda l, i: (l, i, 0, 0))),
        out_shape=(jax.ShapeDtypeStruct((depth, b, LANES, n), BF16), jax.ShapeDtypeStruct((depth, b, heads, LANES), F32)),
        compiler_params=_cparams(("parallel", "parallel")),
        name="cache_scan",
    )(lft)


def _select_head(q2, g, sub):
    lane = lax.broadcasted_iota(jnp.int32, (1, PAIR), 1)
    payload = (lane < LANES) & (lane // FOX_HEAD_DIM == sub)
    side = (lane >= LANES) & ((lane - LANES) // BIAS_SLOTS == 2 * g + sub)
    return jnp.where(payload | side, q2.astype(F32), 0.0).astype(BF16)


def _softmax_pv(s_c, s_n, vt_c, v_n):
    m = jnp.maximum(jnp.max(s_c, axis=1, keepdims=True), jnp.max(s_n, axis=1, keepdims=True))
    p_c = jnp.exp(s_c - m)
    p_n = jnp.exp(s_n - m)
    den = jnp.sum(p_c, axis=1, keepdims=True) + jnp.sum(p_n, axis=1, keepdims=True)
    return (_dot_nt(p_c.astype(BF16), vt_c) + _dot(p_n.astype(BF16), v_n)) / den


def _sample_attn_kernel(kct_ref, vct_ref, kbct_ref, ckvc_ref, krct_ref, wukvt_ref,
                        qf2_ref, kf2_ref, vf2_ref, qm2_ref, km2_ref, vm2_ref,
                        fox_ref, mla_ref, *, past, chunk):
    n = qf2_ref.shape[1]
    row = lax.broadcasted_iota(jnp.int32, (n, n), 0)
    col = lax.broadcasted_iota(jnp.int32, (n, n), 1)
    chunk_ok = ((past + col) // chunk) <= ((past + row) // chunk)
    row2 = lax.broadcasted_iota(jnp.int32, (2 * n, n), 0) % n
    frame_ok2 = lax.broadcasted_iota(jnp.int32, (2 * n, n), 1) <= row2
    first = lax.broadcasted_iota(jnp.int32, (1, LANES), 1) < FOX_HEAD_DIM
    kbct = kbct_ref[0, 0]

    fox_scores = []
    for p in range(FOX_HEADS // 2):
        kkt = jnp.concatenate([kct_ref[0, 0, p * LANES:(p + 1) * LANES].astype(BF16), kbct], axis=0)
        qpair = qf2_ref[0, :, p * PAIR:(p + 1) * PAIR]
        q_st = jnp.concatenate([_select_head(qpair, p, sub) for sub in range(2)], axis=0)
        s_n = _dot_nt(q_st, kf2_ref[0, :, p * PAIR:(p + 1) * PAIR])
        fox_scores.append((_dot(q_st, kkt), jnp.where(frame_ok2, s_n, -1e30)))

    kvt_c = _dot_nt(wukvt_ref[...], ckvc_ref[0, 0].astype(BF16))
    krct = krct_ref[0, 0].astype(BF16)
    pad = jnp.zeros((PAIR - MLA_NOPE_DIM - MLA_ROPE_DIM, krct.shape[1]), BF16)
    mla_scores = []
    for hd in range(MLA_HEADS):
        kkt = jnp.concatenate([kvt_c[hd * LANES:(hd + 1) * LANES].astype(BF16), krct, pad], axis=0)
        q = qm2_ref[0, :, hd * PAIR:(hd + 1) * PAIR]
        s_n = jnp.where(chunk_ok, _dot_nt(q, km2_ref[0, :, hd * PAIR:(hd + 1) * PAIR]), -1e30)
        mla_scores.append((_dot(q, kkt), s_n))

    for p, (s_c, s_n) in enumerate(fox_scores):
        vvt = vct_ref[0, 0, p * LANES:(p + 1) * LANES].astype(BF16)
        o = _softmax_pv(s_c, s_n, vvt, vf2_ref[0, :, p * PAIR:p * PAIR + LANES])
        fox_ref[0, :, p * LANES:(p + 1) * LANES] = jnp.where(first, o[:n], o[n:]).astype(fox_ref.dtype)
    for hd, (s_c, s_n) in enumerate(mla_scores):
        vvt = kvt_c[MLA_HEADS * LANES + hd * LANES:MLA_HEADS * LANES + (hd + 1) * LANES].astype(BF16)
        o = _softmax_pv(s_c, s_n, vvt, vm2_ref[0, :, hd * PAIR:hd * PAIR + LANES])
        mla_ref[0, :, hd * LANES:(hd + 1) * LANES] = o.astype(mla_ref.dtype)


def _sample_attention(l, kct, vct, kbct, ckvc, krct, wukvt, qf2, kf2, vf2, qm2, km2, vm2, *, chunk):
    b, n = qf2.shape[:2]
    past = kct.shape[3]
    cache = lambda a: pl.BlockSpec((1, 1) + a.shape[2:], lambda i: (l, i, 0, 0))
    per_b = lambda a: pl.BlockSpec((1,) + a.shape[1:], lambda i: (i, 0, 0))
    return pl.pallas_call(
        functools.partial(_sample_attn_kernel, past=past, chunk=chunk),
        grid=(b,),
        in_specs=[cache(kct), cache(vct), cache(kbct), cache(ckvc), cache(krct),
                  pl.BlockSpec(wukvt.shape, lambda i: (0, 0)),
                  per_b(qf2), per_b(kf2), per_b(vf2), per_b(qm2), per_b(km2), per_b(vm2)],
        out_specs=(pl.BlockSpec((1, n, FOX_WIDTH), lambda i: (i, 0, 0)),
                   pl.BlockSpec((1, n, MLA_WIDTH), lambda i: (i, 0, 0))),
        out_shape=(jax.ShapeDtypeStruct((b, n, FOX_WIDTH), BF16), jax.ShapeDtypeStruct((b, n, MLA_WIDTH), BF16)),
        compiler_params=_cparams(("parallel",)),
        name="sample_attn",
    )(kct, vct, kbct, ckvc, krct, wukvt, qf2, kf2, vf2, qm2, km2, vm2)


def _memkv_kernel(mem_ref, g_ref, wk_ref, wv_ref, mk_ref, mv_ref, mkb_ref, mvb_ref):
    m = _rms(mem_ref[0], g_ref[0]).astype(BF16)
    mk = _dot(m, wk_ref[0])
    mv = _dot(m, wv_ref[0])
    mk_ref[0, 0] = mk
    mv_ref[0, 0] = mv
    mkb_ref[0, 0] = mk.astype(BF16)
    mvb_ref[0, 0] = mv.astype(BF16)


def _memory_kv(mem, g, wk, wv):
    b, n, d = mem.shape
    depth = g.shape[0]
    o_spec = pl.BlockSpec((1, 1, n, d), lambda l, i: (l, i, 0, 0))
    w_spec = pl.BlockSpec((1, d, d), lambda l, i: (l, 0, 0))
    return pl.pallas_call(
        _memkv_kernel,
        grid=(depth, b),
        in_specs=[pl.BlockSpec((1, n, d), lambda l, i: (i, 0, 0)),
                  pl.BlockSpec((1, 1, d), lambda l, i: (l, 0, 0)), w_spec, w_spec],
        out_specs=(o_spec,) * 4,
        out_shape=(jax.ShapeDtypeStruct((depth, b, n, d), F32),) * 2 + (jax.ShapeDtypeStruct((depth, b, n, d), BF16),) * 2,
        compiler_params=_cparams(("arbitrary", "arbitrary")),
        name="memory_kv",
    )(mem, g, wk, wv)


def _post_kernel(x_ref, fox_ref, mla_ref, wo_ref, g_ref, wxq_ref, mk_ref, mv_ref, wxo_ref, o_ref, *, nb):
    x = x_ref[...] + _dot(fox_ref[...], wo_ref[:FOX_WIDTH, :]) + _dot(mla_ref[...], wo_ref[FOX_WIDTH:, :])
    hc = _rms(x, g_ref[...]).astype(BF16)
    q = (_dot(hc, wxq_ref[...]) * (X_HEAD_DIM ** -0.5)).astype(BF16)
    tm = x.shape[0] // nb
    rows = []
    for bi in range(nb):
        outs = []
        for hd in range(X_HEADS):
            sl = slice(hd * X_HEAD_DIM, (hd + 1) * X_HEAD_DIM)
            s = _dot_nt(q[bi * tm:(bi + 1) * tm, sl], mk_ref[0, bi, :, sl].astype(BF16))
            p = jnp.exp(s - jnp.max(s, axis=1, keepdims=True))
            den = jnp.sum(p, axis=1, keepdims=True)
            outs.append((_dot(p.astype(BF16), mv_ref[0, bi, :, sl].astype(BF16)) / den).astype(BF16))
        rows.append(jnp.concatenate(outs, axis=1))
    o = rows[0] if nb == 1 else jnp.concatenate(rows, axis=0)
    o_ref[...] = x + _dot(o, wxo_ref[...])


def _post(l, x, fox, mla, wo, g, wxq, mk, mv, wxo, *, b, tm, nb):
    n, d = x.shape
    s = n // b
    tiles = s // tm
    rows = nb * tm
    tok = lambda w: pl.BlockSpec((rows, w), lambda i, j: (i * tiles + j, 0))
    full = lambda a: pl.BlockSpec(a.shape, lambda i, j: (0,) * a.ndim)
    mem = pl.BlockSpec((1, nb) + mk.shape[2:], lambda i, j: (l, i) + (0,) * (mk.ndim - 2))
    return pl.pallas_call(
        functools.partial(_post_kernel, nb=nb),
        grid=(b // nb, tiles),
        in_specs=[tok(d), tok(FOX_WIDTH), tok(MLA_WIDTH), full(wo), full(g), full(wxq), mem, mem, full(wxo)],
        out_specs=tok(d),
        out_shape=jax.ShapeDtypeStruct((n, d), F32),
        compiler_params=_cparams(("parallel", "arbitrary")),
        name="post",
    )(x, fox, mla, wo, g, wxq, mk, mv, wxo)


def _mlp_kernel(x_ref, g_ref, wup_ref, wdn_ref, gf_ref, o_ref, *, final, ff_blk):
    x = x_ref[...]
    hm = _rms(x, g_ref[...]).astype(BF16)
    acc = x
    for c in range(D_FF // ff_blk):
        u = jnp.maximum(_dot(hm, wup_ref[:, c * ff_blk:(c + 1) * ff_blk]), 0.0)
        acc = acc + _dot(jnp.square(u).astype(BF16), wdn_ref[c * ff_blk:(c + 1) * ff_blk, :])
    o_ref[...] = _rms(acc, gf_ref[...]) if final else acc


def _mlp(x, g, wup, wdn, gf, *, final, tm):
    n, d = x.shape
    full = lambda a: pl.BlockSpec(a.shape, lambda i: (0,) * a.ndim, pipeline_mode=pl.Buffered(1))
    return pl.pallas_call(
        functools.partial(_mlp_kernel, final=final, ff_blk=1024),
        grid=(n // tm,),
        in_specs=[pl.BlockSpec((tm, d), lambda i: (i, 0)), full(g), full(wup), full(wdn), full(gf)],
        out_specs=pl.BlockSpec((tm, d), lambda i: (i, 0)),
        out_shape=jax.ShapeDtypeStruct((n, d), F32),
        compiler_params=_cparams(("parallel",)),
        name="mlp",
    )(x, g, wup, wdn, gf)


def _gate_block(a):
    pad = jnp.zeros(a.shape[:-1] + (LANES - FOX_HEADS * BIAS_SLOTS - FOX_HEADS,), a.dtype)
    return jnp.concatenate([jnp.repeat(a, BIAS_SLOTS, axis=-1), a, pad], axis=-1)


def _split_win(w):
    o = 3 * FOX_WIDTH
    o2 = o + FOX_HEADS + MLA_Q_RANK
    half = MLA_ROPE_DIM // 2
    kr = w[:, o2 + MLA_KV_RANK:]
    return dict(q=w[:, :FOX_WIDTH], k=w[:, FOX_WIDTH:2 * FOX_WIDTH], v=w[:, 2 * FOX_WIDTH:o], g=w[:, o:o + FOX_HEADS],
                cq=w[:, o + FOX_HEADS:o2], ckv=w[:, o2:o2 + MLA_KV_RANK], kr=kr,
                kr_sw=jnp.concatenate([kr[:, half:], kr[:, :half]], axis=1))


def _layout_win_sample(w):
    p = _split_win(w)
    return jnp.concatenate([p["q"], p["k"], p["v"], p["cq"], p["ckv"], p["kr"], p["kr_sw"], _gate_block(p["g"])], axis=1).astype(BF16)


def _layout_win_prompt(w):
    p = _split_win(w)
    wr = jnp.concatenate([p["k"], p["ckv"], p["kr"], p["kr_sw"], _gate_block(p["g"])], axis=1).astype(BF16)
    wt = jnp.concatenate([p["q"], p["v"], p["cq"], p["ckv"], p["kr"]], axis=1).T.astype(BF16)
    return wr, wt


def _layout_wuq_sample(w):
    w = w.reshape(MLA_Q_RANK, MLA_HEADS, MLA_NOPE_DIM + MLA_ROPE_DIM)
    nope = w[:, :, :MLA_NOPE_DIM].reshape(MLA_Q_RANK, MLA_HEADS * MLA_NOPE_DIM)
    r = w[:, :, MLA_NOPE_DIM:]
    half = MLA_ROPE_DIM // 2
    r2 = jnp.concatenate([r, r[:, :, half:], r[:, :, :half]], axis=2).reshape(MLA_Q_RANK, MLA_HEADS * LANES)
    return jnp.concatenate([nope, r2], axis=1).astype(BF16)


def _layout_wuq_prompt(w):
    w = w.reshape(MLA_Q_RANK, MLA_HEADS, MLA_NOPE_DIM + MLA_ROPE_DIM)
    return jnp.concatenate([w[:, :, :MLA_NOPE_DIM].reshape(MLA_Q_RANK, -1),
                            w[:, :, MLA_NOPE_DIM:].reshape(MLA_Q_RANK, -1)], axis=1).T.astype(BF16)


def _layout_wukv(w):
    w = w.reshape(MLA_KV_RANK, MLA_HEADS, MLA_NOPE_DIM + MLA_V_DIM)
    return jnp.concatenate([w[:, :, :MLA_NOPE_DIM].reshape(MLA_KV_RANK, -1),
                            w[:, :, MLA_NOPE_DIM:].reshape(MLA_KV_RANK, -1)], axis=1).astype(BF16)


def _rope_tables(pos):
    half = MLA_ROPE_DIM // 2
    inv = ROPE_THETA ** (-jnp.arange(half, dtype=F32) / half)
    ang = pos.astype(F32)[:, None] * inv[None, :]
    return jnp.cos(ang), jnp.sin(ang)


def _tile(n, pref):
    return pref if n % pref == 0 else n


def kernel(x_prompt, x_sample, mem_prompt, cache_fox_k, cache_fox_v, cache_fox_logf, cache_mla_ckv, cache_mla_krope,
           cache_mem_k, cache_mem_v, norm_mix, w_in, b_forget, mla_q_norm, w_uq, mla_kv_norm, w_ukv, w_out,
           norm_cross, norm_mem, w_xq, w_mk, w_mv, w_xo, norm_mlp, w_up, w_down, norm_final):
    depth = w_in.shape[0]
    bp, sp, d = x_prompt.shape
    bs, ss, _ = x_sample.shape
    past = cache_fox_k.shape[2]
    n_mem = mem_prompt.shape[1]
    row = lambda a: a.reshape(1, -1)
    col = lambda a: a.reshape(-1, 1)

    cos_p, sin_p = _rope_tables(jnp.arange(sp))
    cos_s, sin_s = _rope_tables(past + jnp.arange(ss))
    cs_p = jnp.concatenate([cos_p, cos_p, -sin_p, sin_p], axis=1)
    cs_s = jnp.tile(jnp.concatenate([cos_s, cos_s, -sin_s, sin_s], axis=1), (bs, 1))
    cos_pt, sin_pt = cos_p.T, sin_p.T

    mk_all, mv_all, mkb_all, mvb_all = _memory_kv(mem_prompt, norm_mem[:, None, :], w_mk.astype(BF16), w_mv.astype(BF16))

    kct = jnp.transpose(cache_fox_k, (0, 1, 3, 4, 2)).reshape(depth, bs, FOX_WIDTH, past)
    vct = jnp.transpose(cache_fox_v, (0, 1, 3, 4, 2)).reshape(depth, bs, FOX_WIDTH, past)
    krct = jnp.transpose(cache_mla_krope, (0, 1, 3, 2))
    kbct, tot = _cache_scan(jnp.transpose(cache_fox_logf, (0, 1, 3, 2)).astype(F32))
    cin_all = _gate_block(tot[..., 0])[:, :, None, :]

    cmk = cache_mem_k.reshape(depth, bs, n_mem, d)
    cmv = cache_mem_v.reshape(depth, bs, n_mem, d)

    tm_p = _tile(sp, 512)
    xp, xs = x_prompt, x_sample
    kt_all = vt_all = None
    rows_p, rows_s = [], []
    for l in range(depth):
        wsplit_s = _layout_win_sample(w_in[l])
        wr, wt = _layout_win_prompt(w_in[l])
        bblk = _gate_block(b_forget[l][None, :])
        wukv = _layout_wukv(w_ukv[l])
        wnope, wvt = wukv[:, :MLA_HEADS * MLA_NOPE_DIM], wukv[:, MLA_HEADS * MLA_NOPE_DIM:].T
        wo, wxq, wxo = w_out[l].astype(BF16), w_xq[l].astype(BF16), w_xo[l].astype(BF16)
        wup, wdn = w_up[l].astype(BF16), w_down[l].astype(BF16)
        gmix, gq, gkv, gx, gm = row(norm_mix[l]), row(mla_q_norm[l]), row(mla_kv_norm[l]), row(norm_cross[l]), row(norm_mlp[l])
        final = l == depth - 1

        kt_all, vt_all, lft, ckv, krt, qt, k2, vt2, qmt, km2, vmt = _inproj_prompt(
            xp, cs_p, cos_pt, sin_pt, gmix, wr, wt, bblk, col(mla_q_norm[l]), _layout_wuq_prompt(w_uq[l]),
            gkv, col(mla_kv_norm[l]), wnope, wvt, kt_all, vt_all, tm=tm_p)
        fox = _attention(qt, k2, vt2, n_chain=4, n_sub=2, chunk=1, v_rows=FOX_V_ROWS, v_dim=FOX_HEAD_DIM)
        mla = _attention(qmt, km2, vmt, n_chain=2, n_sub=1, chunk=CHUNK, v_rows=MLA_V_ROWS, v_dim=MLA_V_DIM)
        xp = _post(l, xp.reshape(bp * sp, d), fox.reshape(bp * sp, FOX_WIDTH), mla.reshape(bp * sp, MLA_WIDTH),
                   wo, gx, wxq, mkb_all, mvb_all, wxo, b=bp, tm=tm_p, nb=1)
        xp = _mlp(xp, gm, wup, wdn, row(norm_final), final=final, tm=_tile(bp * sp, 512)).reshape(bp, sp, d)
        rows_p.append((lft, ckv, krt))

        ns = bs * ss
        outs = _inproj(xs.reshape(1, ns, d), cin_all[l], cs_s, gmix, wsplit_s, bblk, gq, _layout_wuq_sample(w_uq[l]),
                       gkv, wukv, tm=_tile(ns, 8 * ss), seg=ss)
        kf, vf, lf, ckv, kr, qf2, kf2, vf2, qm2, km2, vm2 = (o.reshape(bs, ss, o.shape[-1]) for o in outs)
        fox, mla = _sample_attention(l, kct, vct, kbct, cache_mla_ckv, krct, wukv.T, qf2, kf2, vf2, qm2, km2, vm2, chunk=CHUNK)
        xs = _post(l, xs.reshape(ns, d), fox.reshape(ns, FOX_WIDTH), mla.reshape(ns, MLA_WIDTH),
                   wo, gx, wxq, cmk, cmv, wxo, b=bs, tm=ss, nb=4 if bs % 4 == 0 else 1)
        xs = _mlp(xs, gm, wup, wdn, row(norm_final), final=final, tm=_tile(ns, 512)).reshape(bs, ss, d)
        rows_s.append((kf, vf, lf, ckv, kr))

    st = lambda rows, i: jnp.stack([r[i] for r in rows])
    heads5 = lambda a: jnp.transpose(a.reshape(depth, bp, FOX_HEADS, FOX_HEAD_DIM, sp), (0, 1, 4, 2, 3))
    lo = FOX_HEADS * BIAS_SLOTS
    mem_shape = (depth, bp, n_mem, X_HEADS, X_HEAD_DIM)
    return (xp, xs,
            heads5(kt_all), heads5(vt_all), jnp.transpose(st(rows_p, 0), (0, 1, 3, 2)), st(rows_p, 1),
            jnp.transpose(st(rows_p, 2), (0, 1, 3, 2)),
            mk_all.reshape(mem_shape), mv_all.reshape(mem_shape),
            st(rows_s, 0).reshape(depth, bs, ss, FOX_HEADS, FOX_HEAD_DIM), st(rows_s, 1).reshape(depth, bs, ss, FOX_HEADS, FOX_HEAD_DIM),
            st(rows_s, 2)[..., lo:lo + FOX_HEADS], st(rows_s, 3), st(rows_s, 4))
```

```python
import functools

import jax
import jax.numpy as jnp
from jax import lax
from jax.experimental import pallas as pl
from jax.experimental.pallas import tpu as pltpu

F32 = jnp.float32
BF16 = jnp.bfloat16

D_MODEL = 1024
CHUNK = 64
FOX_HEADS = 8
FOX_HEAD_DIM = 64
FOX_WIDTH = FOX_HEADS * FOX_HEAD_DIM
MLA_HEADS = 4
MLA_NOPE_DIM = 128
MLA_ROPE_DIM = 64
MLA_V_DIM = 128
MLA_WIDTH = MLA_HEADS * MLA_V_DIM
MLA_Q_RANK = 256
MLA_KV_RANK = 128
ROPE_THETA = 10000.0
X_HEADS = 4
X_HEAD_DIM = 256
D_FF = 4 * D_MODEL
EPS = 1e-6

LOG2E = 1.4426950408889634
LANES = 128
SUBLANES = 8
BF16_ROWS = 16
PAIR = 2 * LANES
BIAS_SLOTS = 8
VMEM_LIMIT = 56 * 1024 * 1024
FOX_V_ROWS = FOX_HEAD_DIM + BF16_ROWS
MLA_V_ROWS = MLA_V_DIM + BF16_ROWS

_OQ, _OK, _OV = 0, FOX_WIDTH, 2 * FOX_WIDTH
_OCQ = 3 * FOX_WIDTH
_OCKV = _OCQ + MLA_Q_RANK
_OKR = _OCKV + MLA_KV_RANK
_OG = _OKR + LANES
_RK, _RCKV, _RKR, _RG = 0, FOX_WIDTH, FOX_WIDTH + MLA_KV_RANK, FOX_WIDTH + MLA_KV_RANK + LANES
_TQ, _TV = 0, FOX_WIDTH
_TCQ = 2 * FOX_WIDTH
_TCKV = _TCQ + MLA_Q_RANK
_TKR = _TCKV + MLA_KV_RANK


def _cparams(sem):
    return pltpu.CompilerParams(dimension_semantics=sem, vmem_limit_bytes=VMEM_LIMIT)


def _rms(x, g):
    return x * lax.rsqrt(jnp.mean(jnp.square(x), axis=-1, keepdims=True) + EPS) * g


def _rms_rows(x, g):
    return x * lax.rsqrt(jnp.mean(jnp.square(x), axis=0, keepdims=True) + EPS) * g


def _dot(a, b):
    return jnp.dot(a, b, preferred_element_type=F32)


def _dot_nt(a, b):
    return lax.dot_general(a, b, (((1,), (1,)), ((), ())), preferred_element_type=F32)


def _split3(x):
    x1 = x.astype(BF16)
    r1 = x - x1.astype(F32)
    x2 = r1.astype(BF16)
    r2 = r1 - x2.astype(F32)
    return x1, x2, r2.astype(BF16)


def _cumsum_rows(x, carry):
    n = x.shape[0]
    blk = min(n, 256)
    r = lax.broadcasted_iota(jnp.int32, (blk, blk), 0)
    c = lax.broadcasted_iota(jnp.int32, (blk, blk), 1)
    tri = (c <= r).astype(BF16)
    outs = []
    for i in range(n // blk):
        x1, x2, x3 = _split3(x[i * blk:(i + 1) * blk])
        cb = _dot(tri, x1) + _dot(tri, x2) + _dot(tri, x3) + carry
        carry = cb[blk - 1:blk]
        outs.append(cb)
    return outs[0] if len(outs) == 1 else jnp.concatenate(outs, axis=0)


def _cumsum_lanes(x):
    n = x.shape[1]
    blk = min(n, 256)
    r = lax.broadcasted_iota(jnp.int32, (blk, blk), 0)
    c = lax.broadcasted_iota(jnp.int32, (blk, blk), 1)
    tri = (r <= c).astype(BF16)
    carry = jnp.zeros((x.shape[0], 1), F32)
    outs = []
    for i in range(n // blk):
        x1, x2, x3 = _split3(x[:, i * blk:(i + 1) * blk])
        cb = _dot(x1, tri) + _dot(x2, tri) + _dot(x3, tri) + carry
        carry = cb[:, blk - 1:blk]
        outs.append(cb)
    return outs[0] if len(outs) == 1 else jnp.concatenate(outs, axis=1)


def _bias_blocks(cum, slot_axis):
    c1, c2, c3 = (c.astype(F32) for c in _split3(cum))
    slot = lax.broadcasted_iota(jnp.int32, cum.shape, slot_axis)
    j = slot % BIAS_SLOTS
    valid = slot < FOX_HEADS * BIAS_SLOTS
    one = jnp.where(valid & (j < 6), 1.0, 0.0)
    qb = jnp.where(j == 0, c1, jnp.where(j == 1, c2, jnp.where(j == 2, c3, one)))
    kb = jnp.where(j < 3, one, jnp.where(j == 3, -c1, jnp.where(j == 4, -c2, jnp.where(j == 5, -c3, 0.0))))
    return jnp.where(valid, qb, 0.0), jnp.where(valid, kb, 0.0)


def _log_sigmoid(x):
    return jnp.minimum(x, 0.0) - jnp.log1p(jnp.exp(-jnp.abs(x)))


def _rope_pair(t, cs):
    u = t * cs
    return u + pltpu.roll(u, MLA_ROPE_DIM, 1)


def _rope_rows(x, cos, sin):
    half = MLA_ROPE_DIM // 2
    x1, x2 = x[:half], x[half:]
    return jnp.concatenate([x1 * cos - x2 * sin, x2 * cos + x1 * sin], axis=0)


def _inproj_kernel(x_ref, cin_ref, cs_ref, gmix_ref, win_ref, bblk_ref, gq_ref, wuq_ref, gkv_ref, wukv_ref,
                   kf_ref, vf_ref, lf_ref, ckv_ref, kr_ref,
                   qf2_ref, kf2_ref, vf2_ref, qm2_ref, km2_ref, vm2_ref, *, seg):
    x = x_ref[0]
    tm = x.shape[0]
    h = _rms(x, gmix_ref[...])
    z = _dot(h.astype(BF16), win_ref[...])
    cs = cs_ref[...]
    lane = lax.broadcasted_iota(jnp.int32, (1, LANES), 1)
    low = lane < MLA_ROPE_DIM
    ones_blk = jnp.ones((tm, LANES), BF16)

    kf_ref[0] = z[:, _OK:_OK + FOX_WIDTH]
    vf_ref[0] = z[:, _OV:_OV + FOX_WIDTH]

    lf = _log_sigmoid(z[:, _OG:_OG + LANES] + bblk_ref[...])
    lf_ref[0] = lf
    cums = [_cumsum_rows(lf[i * seg:(i + 1) * seg], cin_ref[i]) for i in range(tm // seg)]
    cum = cums[0] if len(cums) == 1 else jnp.concatenate(cums, axis=0)
    qb, kb = _bias_blocks(cum, 1)

    fox_scale = FOX_HEAD_DIM ** -0.5
    for p in range(FOX_HEADS // 2):
        lo, hi = p * PAIR, p * PAIR + LANES
        qf2_ref[0, :, lo:hi] = (z[:, _OQ + p * LANES:_OQ + (p + 1) * LANES] * fox_scale).astype(BF16)
        qf2_ref[0, :, hi:hi + LANES] = qb.astype(BF16)
        kf2_ref[0, :, lo:hi] = z[:, _OK + p * LANES:_OK + (p + 1) * LANES].astype(BF16)
        kf2_ref[0, :, hi:hi + LANES] = kb.astype(BF16)
        vf2_ref[0, :, lo:hi] = z[:, _OV + p * LANES:_OV + (p + 1) * LANES].astype(BF16)
        vf2_ref[0, :, hi:hi + LANES] = ones_blk

    kr = _rope_pair(z[:, _OKR:_OKR + LANES], cs)
    kr_ref[0] = kr[:, :MLA_ROPE_DIM]
    krb = jnp.where(low, kr, 0.0).astype(BF16)

    ckv = _rms(z[:, _OCKV:_OCKV + MLA_KV_RANK], gkv_ref[...])
    ckv_ref[0] = ckv
    kv = _dot(ckv.astype(BF16), wukv_ref[...])
    cqn = _rms(z[:, _OCQ:_OCQ + MLA_Q_RANK], gq_ref[...])
    qm = _dot(cqn.astype(BF16), wuq_ref[...])
    mla_scale = (MLA_NOPE_DIM + MLA_ROPE_DIM) ** -0.5
    for hd in range(MLA_HEADS):
        lo, hi = hd * PAIR, hd * PAIR + LANES
        qr = _rope_pair(qm[:, MLA_HEADS * LANES + hd * LANES:MLA_HEADS * LANES + (hd + 1) * LANES], cs)
        qm2_ref[0, :, lo:hi] = (qm[:, hd * LANES:(hd + 1) * LANES] * mla_scale).astype(BF16)
        qm2_ref[0, :, hi:hi + LANES] = (jnp.where(low, qr, 0.0) * mla_scale).astype(BF16)
        km2_ref[0, :, lo:hi] = kv[:, hd * LANES:(hd + 1) * LANES].astype(BF16)
        km2_ref[0, :, hi:hi + LANES] = krb
        vm2_ref[0, :, lo:hi] = kv[:, MLA_HEADS * LANES + hd * LANES:MLA_HEADS * LANES + (hd + 1) * LANES].astype(BF16)
        vm2_ref[0, :, hi:hi + LANES] = ones_blk


def _inproj(x, cin, cs, gmix, win, bblk, gq, wuq, gkv, wukv, *, tm, seg):
    b, s, _ = x.shape
    grid = (b, s // tm)
    tok = lambda w: pl.BlockSpec((1, tm, w), lambda i, j: (i, j, 0))
    full = lambda a: pl.BlockSpec(a.shape, lambda i, j: (0,) * a.ndim)
    out_shape = (
        jax.ShapeDtypeStruct((b, s, FOX_WIDTH), F32), jax.ShapeDtypeStruct((b, s, FOX_WIDTH), F32),
        jax.ShapeDtypeStruct((b, s, LANES), F32), jax.ShapeDtypeStruct((b, s, MLA_KV_RANK), F32),
        jax.ShapeDtypeStruct((b, s, MLA_ROPE_DIM), F32),
    ) + tuple(jax.ShapeDtypeStruct((b, s, 4 * PAIR), BF16) for _ in range(6))
    out_specs = (tok(FOX_WIDTH), tok(FOX_WIDTH), tok(LANES), tok(MLA_KV_RANK), tok(MLA_ROPE_DIM)) + (tok(4 * PAIR),) * 6
    return pl.pallas_call(
        functools.partial(_inproj_kernel, seg=seg),
        grid=grid,
        in_specs=[tok(D_MODEL), pl.BlockSpec((tm // seg, 1, LANES), lambda i, j: (j, 0, 0)),
                  pl.BlockSpec((tm, LANES), lambda i, j: (j, 0)),
                  full(gmix), full(win), full(bblk), full(gq), full(wuq), full(gkv), full(wukv)],
        out_specs=out_specs,
        out_shape=out_shape,
        compiler_params=_cparams(("parallel", "parallel")),
        name="inproj_sample",
    )(x, cin, cs, gmix, win, bblk, gq, wuq, gkv, wukv)


def _inproj_prompt_kernel(*refs, n_prev):
    (x_ref, cs_ref, cos_ref, sin_ref, gmix_ref, wr_ref, wt_ref, bblk_ref, gq_ref, wuqt_ref, gkv_ref, gkvc_ref,
     wnope_ref, wvt_ref) = refs[:14]
    prev_k_ref, prev_v_ref = (refs[14], refs[15]) if n_prev else (None, None)
    (kt_ref, vt_ref, lft_ref, ckv_ref, krt_ref,
     qt_ref, k2_ref, vt2_ref, qmt_ref, km2_ref, vmt_ref, carry_ref) = refs[14 + (2 if n_prev else 0):]

    @pl.when(pl.program_id(1) == 0)
    def _():
        carry_ref[...] = jnp.zeros_like(carry_ref)

    x = x_ref[0]
    tm = x.shape[0]
    hb = _rms(x, gmix_ref[...]).astype(BF16)
    zr = _dot(hb, wr_ref[...])
    zt = _dot_nt(wt_ref[...], hb)
    cs, cos, sin = cs_ref[...], cos_ref[...], sin_ref[...]
    ones_rows = jnp.ones((BF16_ROWS, tm), BF16)

    if n_prev:
        kt_ref[:n_prev, 0] = prev_k_ref[:, 0]
        vt_ref[:n_prev, 0] = prev_v_ref[:, 0]
    kt_ref[n_prev, 0] = zr[:, _RK:_RK + FOX_WIDTH].T
    vt_ref[n_prev, 0] = zt[_TV:_TV + FOX_WIDTH]

    lf = _log_sigmoid(zr[:, _RG:_RG + LANES] + bblk_ref[...])
    cum = _cumsum_rows(lf, carry_ref[...])
    carry_ref[...] = cum[tm - 1:tm]
    qb, kb = _bias_blocks(cum * LOG2E, 1)
    kb = kb.astype(BF16)
    qbt = qb.T
    lo = FOX_HEADS * BIAS_SLOTS
    lft_ref[0] = lf.T[lo:lo + FOX_HEADS]

    fox_scale = FOX_HEAD_DIM ** -0.5 * LOG2E
    rowi = lax.broadcasted_iota(jnp.int32, (LANES, tm), 0)
    for p in range(FOX_HEADS // 2):
        zq = zt[_TQ + p * LANES:_TQ + (p + 1) * LANES] * fox_scale
        for sub in range(2):
            hd = 2 * p + sub
            qt_ref[0, 0, hd * PAIR:hd * PAIR + LANES] = jnp.where(rowi // FOX_HEAD_DIM == sub, zq, 0.0).astype(BF16)
            qt_ref[0, 0, hd * PAIR + LANES:(hd + 1) * PAIR] = jnp.where(rowi // BIAS_SLOTS == hd, qbt, 0.0).astype(BF16)
        k2_ref[0, :, p * PAIR:p * PAIR + LANES] = zr[:, _RK + p * LANES:_RK + (p + 1) * LANES].astype(BF16)
        k2_ref[0, :, p * PAIR + LANES:(p + 1) * PAIR] = kb
    for hd in range(FOX_HEADS):
        r0 = hd * FOX_V_ROWS
        vt2_ref[0, 0, r0:r0 + FOX_HEAD_DIM] = zt[_TV + hd * FOX_HEAD_DIM:_TV + (hd + 1) * FOX_HEAD_DIM].astype(BF16)
        vt2_ref[0, 0, r0 + FOX_HEAD_DIM:r0 + FOX_V_ROWS] = ones_rows

    krt_ref[0] = _rope_rows(zt[_TKR:_TKR + MLA_ROPE_DIM], cos, sin)
    kr = _rope_pair(zr[:, _RKR:_RKR + LANES], cs)
    krb = jnp.where(lax.broadcasted_iota(jnp.int32, (1, LANES), 1) < MLA_ROPE_DIM, kr, 0.0).astype(BF16)
    ckv = _rms(zr[:, _RCKV:_RCKV + MLA_KV_RANK], gkv_ref[...])
    ckv_ref[0] = ckv
    knope = _dot(ckv.astype(BF16), wnope_ref[...])
    ckvt = _rms_rows(zt[_TCKV:_TCKV + MLA_KV_RANK], gkvc_ref[...])
    vmt = _dot(wvt_ref[...], ckvt.astype(BF16))
    cqnt = _rms_rows(zt[_TCQ:_TCQ + MLA_Q_RANK], gq_ref[...])
    qmt = _dot(wuqt_ref[...], cqnt.astype(BF16))
    mla_scale = (MLA_NOPE_DIM + MLA_ROPE_DIM) ** -0.5 * LOG2E
    zeros_rows = jnp.zeros((PAIR - MLA_NOPE_DIM - MLA_ROPE_DIM, tm), BF16)
    for hd in range(MLA_HEADS):
        q0 = hd * PAIR
        qmt_ref[0, 0, q0:q0 + MLA_NOPE_DIM] = (qmt[hd * MLA_NOPE_DIM:(hd + 1) * MLA_NOPE_DIM] * mla_scale).astype(BF16)
        r0 = MLA_HEADS * MLA_NOPE_DIM + hd * MLA_ROPE_DIM
        qmt_ref[0, 0, q0 + MLA_NOPE_DIM:q0 + MLA_NOPE_DIM + MLA_ROPE_DIM] = (
            _rope_rows(qmt[r0:r0 + MLA_ROPE_DIM], cos, sin) * mla_scale).astype(BF16)
        qmt_ref[0, 0, q0 + MLA_NOPE_DIM + MLA_ROPE_DIM:q0 + PAIR] = zeros_rows
        km2_ref[0, :, q0:q0 + LANES] = knope[:, hd * LANES:(hd + 1) * LANES].astype(BF16)
        km2_ref[0, :, q0 + LANES:q0 + PAIR] = krb
        v0 = hd * MLA_V_ROWS
        vmt_ref[0, 0, v0:v0 + MLA_V_DIM] = vmt[hd * MLA_V_DIM:(hd + 1) * MLA_V_DIM].astype(BF16)
        vmt_ref[0, 0, v0 + MLA_V_DIM:v0 + MLA_V_ROWS] = ones_rows


def _inproj_prompt(x, cs, cos_t, sin_t, gmix, wr, wt, bblk, gq_col, wuqt, gkv, gkv_col, wnope, wvt, prev_k, prev_v, *, tm):
    b, s, _ = x.shape
    nt = s // tm
    n_prev = 0 if prev_k is None else prev_k.shape[0]
    full = lambda a: pl.BlockSpec(a.shape, lambda i, j: (0,) * a.ndim, pipeline_mode=pl.Buffered(1))
    tile4 = lambda rows: pl.BlockSpec((1, 1, rows, tm), lambda i, j: (i, j, 0, 0))
    tok = lambda w: pl.BlockSpec((1, tm, w), lambda i, j: (i, j, 0))
    feat = lambda rows: pl.BlockSpec((1, rows, tm), lambda i, j: (i, 0, j))
    stacked = lambda n: pl.BlockSpec((n, 1, FOX_WIDTH, tm), lambda i, j: (0, i, 0, j))
    ins = [x, cs, cos_t, sin_t, gmix, wr, wt, bblk, gq_col, wuqt, gkv, gkv_col, wnope, wvt]
    in_specs = [tok(D_MODEL), pl.BlockSpec((tm, LANES), lambda i, j: (j, 0)),
                pl.BlockSpec((cos_t.shape[0], tm), lambda i, j: (0, j)),
                pl.BlockSpec((sin_t.shape[0], tm), lambda i, j: (0, j))] + [full(a) for a in ins[4:]]
    if n_prev:
        ins += [prev_k, prev_v]
        in_specs += [stacked(n_prev), stacked(n_prev)]
    op4 = lambda rows: jax.ShapeDtypeStruct((b, nt, rows, tm), BF16)
    out_shape = (jax.ShapeDtypeStruct((n_prev + 1, b, FOX_WIDTH, s), F32),) * 2 + (
        jax.ShapeDtypeStruct((b, FOX_HEADS, s), F32), jax.ShapeDtypeStruct((b, s, MLA_KV_RANK), F32),
        jax.ShapeDtypeStruct((b, MLA_ROPE_DIM, s), F32),
        op4(FOX_HEADS * PAIR), jax.ShapeDtypeStruct((b, s, 4 * PAIR), BF16), op4(FOX_HEADS * FOX_V_ROWS),
        op4(MLA_HEADS * PAIR), jax.ShapeDtypeStruct((b, s, 4 * PAIR), BF16), op4(MLA_HEADS * MLA_V_ROWS))
    out_specs = (stacked(n_prev + 1),) * 2 + (
        feat(FOX_HEADS), tok(MLA_KV_RANK), feat(MLA_ROPE_DIM),
        tile4(FOX_HEADS * PAIR), tok(4 * PAIR), tile4(FOX_HEADS * FOX_V_ROWS),
        tile4(MLA_HEADS * PAIR), tok(4 * PAIR), tile4(MLA_HEADS * MLA_V_ROWS))
    return pl.pallas_call(
        functools.partial(_inproj_prompt_kernel, n_prev=n_prev),
        grid=(b, nt),
        in_specs=in_specs,
        out_specs=out_specs,
        out_shape=out_shape,
        scratch_shapes=[pltpu.VMEM((1, LANES), F32)],
        compiler_params=_cparams(("parallel", "arbitrary")),
        name="inproj_prompt",
    )(*ins)


def _attn_kernel(qt_ref, k_ref, vt_ref, o_ref, sa_ref, sb_ref, sd_ref, m_ref, acc_ref, *,
                 t, nt, n_chain, n_sub, chunk, v_rows, v_dim):
    i = pl.program_id(2)
    key = lax.broadcasted_iota(jnp.int32, (t, t), 0)
    qry = lax.broadcasted_iota(jnp.int32, (t, t), 1)
    allowed = (key // chunk) <= (qry // chunk)

    def scores(qblk, kblk, dst_ref, diagonal=False):
        off = pl.multiple_of(kblk * t, t)
        for c in range(n_chain):
            g = c // n_sub
            st = _dot(k_ref[0, pl.ds(off, t), g * PAIR:(g + 1) * PAIR], qt_ref[0, qblk, c * PAIR:(c + 1) * PAIR])
            if diagonal:
                st = jnp.where(allowed, st, -1e30)
            dst_ref[c, :t] = st
            dst_ref[c, t:] = jnp.broadcast_to(jnp.max(st, axis=0, keepdims=True), (SUBLANES, t))

    def consume(kblk, src_ref):
        for c in range(n_chain):
            st = src_ref[c, :t]
            m = m_ref[c]
            m_new = jnp.maximum(m, src_ref[c, t:t + 1])
            pt = jnp.exp2(st - m_new).astype(BF16)
            acc_ref[c] = jnp.exp2(m - m_new) * acc_ref[c] + _dot(vt_ref[0, kblk, c * v_rows:(c + 1) * v_rows], pt)
            m_ref[c] = m_new

    def next_diagonal():
        nxt = jnp.minimum(i + 1, nt - 1)
        scores(nxt, nxt, sd_ref, diagonal=True)

    @pl.when(i == 0)
    def _():
        scores(0, 0, sd_ref, diagonal=True)

    m_ref[...] = jnp.full(m_ref.shape, -1e30, F32)
    acc_ref[...] = jnp.zeros(acc_ref.shape, F32)

    @pl.when(i == 0)
    def _():
        consume(0, sd_ref)
        next_diagonal()

    @pl.when(i > 0)
    def _():
        scores(i, 0, sa_ref)
        consume(i, sd_ref)
        n_pairs = (i - 1) // 2

        def pair(p, carry):
            scores(i, 2 * p + 1, sb_ref)
            consume(2 * p, sa_ref)
            scores(i, 2 * p + 2, sa_ref)
            consume(2 * p + 1, sb_ref)
            return carry

        lax.fori_loop(0, n_pairs, pair, 0)
        one_more = (i - 1) % 2 == 1

        @pl.when(one_more)
        def _():
            scores(i, i - 1, sb_ref)
            consume(2 * n_pairs, sa_ref)
            next_diagonal()
            consume(i - 1, sb_ref)

        @pl.when(jnp.logical_not(one_more))
        def _():
            next_diagonal()
            consume(2 * n_pairs, sa_ref)

    for g in range(n_chain // n_sub):
        outs = []
        for sub in range(n_sub):
            acc = acc_ref[g * n_sub + sub]
            outs.append(acc[:v_dim] / acc[v_dim:v_dim + 1])
        ot = outs[0] if n_sub == 1 else jnp.concatenate(outs, axis=0)
        o_ref[0, :, g * LANES:(g + 1) * LANES] = ot.T.astype(o_ref.dtype)


def _attention(qt, k2, vt, *, n_chain, n_sub, chunk, v_rows, v_dim):
    b, nt, rows, t = qt.shape
    s = nt * t
    steps = rows // (n_chain * PAIR)
    gps = n_chain // n_sub
    return pl.pallas_call(
        functools.partial(_attn_kernel, t=t, nt=nt, n_chain=n_chain, n_sub=n_sub, chunk=chunk, v_rows=v_rows, v_dim=v_dim),
        grid=(b, steps, nt),
        in_specs=[pl.BlockSpec((1, nt, n_chain * PAIR, t), lambda bi, g, i: (bi, 0, g, 0)),
                  pl.BlockSpec((1, s, gps * PAIR), lambda bi, g, i: (bi, 0, g)),
                  pl.BlockSpec((1, nt, n_chain * v_rows, t), lambda bi, g, i: (bi, 0, g, 0),
                               pipeline_mode=pl.Buffered(1))],
        out_specs=pl.BlockSpec((1, t, gps * LANES), lambda bi, g, i: (bi, i, g)),
        out_shape=jax.ShapeDtypeStruct((b, s, 4 * LANES), BF16),
        scratch_shapes=[pltpu.VMEM((n_chain, t + SUBLANES, t), F32)] * 3
                       + [pltpu.VMEM((n_chain, 1, t), F32), pltpu.VMEM((n_chain, v_rows, t), F32)],
        compiler_params=_cparams(("parallel", "arbitrary", "arbitrary")),
        name="attn_fox" if n_sub > 1 else "attn_mla",
    )(qt, k2, vt)


def _cache_scan_kernel(lft_ref, kbct_ref, tot_ref):
    lf = lft_ref[0, 0]
    heads, n = lf.shape
    cum = _cumsum_lanes(jnp.concatenate([lf, jnp.zeros_like(lf)], axis=0))[:heads]
    tot_ref[0, 0] = jnp.broadcast_to(cum[:, n - 1:n], (heads, LANES))
    rep = jnp.concatenate([jnp.broadcast_to(cum[hd:hd + 1], (BIAS_SLOTS, n)) for hd in range(heads)]
                          + [jnp.zeros((LANES - heads * BIAS_SLOTS, n), F32)], axis=0)
    kbct_ref[0, 0] = _bias_blocks(rep, 0)[1].astype(BF16)


def _cache_scan(lft):
    depth, b, heads, n = lft.shape
    return pl.pallas_call(
        _cache_scan_kernel,
        grid=(depth, b),
        in_specs=[pl.BlockSpec((1, 1, heads, n), lambda l, i: (l, i, 0, 0))],
        out_specs=(pl.BlockSpec((1, 1, LANES, n), lambda l, i: (l, i, 0, 0)),
                   pl.BlockSpec((1, 1, heads, LANES), lambda l, i: (l, i, 0, 0))),
        out_shape=(jax.ShapeDtypeStruct((depth, b, LANES, n), BF16), jax.ShapeDtypeStruct((depth, b, heads, LANES), F32)),
        compiler_params=_cparams(("parallel", "parallel")),
        name="cache_scan",
    )(lft)


def _select_head(q2, g, sub):
    lane = lax.broadcasted_iota(jnp.int32, (1, PAIR), 1)
    payload = (lane < LANES) & (lane // FOX_HEAD_DIM == sub)
    side = (lane >= LANES) & ((lane - LANES) // BIAS_SLOTS == 2 * g + sub)
    return jnp.where(payload | side, q2.astype(F32), 0.0).astype(BF16)


def _softmax_pv(s_c, s_n, vt_c, v_n):
    m = jnp.maximum(jnp.max(s_c, axis=1, keepdims=True), jnp.max(s_n, axis=1, keepdims=True))
    p_c = jnp.exp(s_c - m)
    p_n = jnp.exp(s_n - m)
    den = jnp.sum(p_c, axis=1, keepdims=True) + jnp.sum(p_n, axis=1, keepdims=True)
    return (_dot_nt(p_c.astype(BF16), vt_c) + _dot(p_n.astype(BF16), v_n)) / den


def _sample_attn_kernel(kct_ref, vct_ref, kbct_ref, ckvc_ref, krct_ref, wukvt_ref,
                        qf2_ref, kf2_ref, vf2_ref, qm2_ref, km2_ref, vm2_ref,
                        fox_ref, mla_ref, *, past, chunk):
    n = qf2_ref.shape[1]
    row = lax.broadcasted_iota(jnp.int32, (n, n), 0)
    col = lax.broadcasted_iota(jnp.int32, (n, n), 1)
    chunk_ok = ((past + col) // chunk) <= ((past + row) // chunk)
    row2 = lax.broadcasted_iota(jnp.int32, (2 * n, n), 0) % n
    frame_ok2 = lax.broadcasted_iota(jnp.int32, (2 * n, n), 1) <= row2
    first = lax.broadcasted_iota(jnp.int32, (1, LANES), 1) < FOX_HEAD_DIM
    kbct = kbct_ref[0, 0]

    fox_scores = []
    for p in range(FOX_HEADS // 2):
        kkt = jnp.concatenate([kct_ref[0, 0, p * LANES:(p + 1) * LANES].astype(BF16), kbct], axis=0)
        qpair = qf2_ref[0, :, p * PAIR:(p + 1) * PAIR]
        q_st = jnp.concatenate([_select_head(qpair, p, sub) for sub in range(2)], axis=0)
        s_n = _dot_nt(q_st, kf2_ref[0, :, p * PAIR:(p + 1) * PAIR])
        fox_scores.append((_dot(q_st, kkt), jnp.where(frame_ok2, s_n, -1e30)))

    kvt_c = _dot_nt(wukvt_ref[...], ckvc_ref[0, 0].astype(BF16))
    krct = krct_ref[0, 0].astype(BF16)
    pad = jnp.zeros((PAIR - MLA_NOPE_DIM - MLA_ROPE_DIM, krct.shape[1]), BF16)
    mla_scores = []
    for hd in range(MLA_HEADS):
        kkt = jnp.concatenate([kvt_c[hd * LANES:(hd + 1) * LANES].astype(BF16), krct, pad], axis=0)
        q = qm2_ref[0, :, hd * PAIR:(hd + 1) * PAIR]
        s_n = jnp.where(chunk_ok, _dot_nt(q, km2_ref[0, :, hd * PAIR:(hd + 1) * PAIR]), -1e30)
        mla_scores.append((_dot(q, kkt), s_n))

    for p, (s_c, s_n) in enumerate(fox_scores):
        vvt = vct_ref[0, 0, p * LANES:(p + 1) * LANES].astype(BF16)
        o = _softmax_pv(s_c, s_n, vvt, vf2_ref[0, :, p * PAIR:p * PAIR + LANES])
        fox_ref[0, :, p * LANES:(p + 1) * LANES] = jnp.where(first, o[:n], o[n:]).astype(fox_ref.dtype)
    for hd, (s_c, s_n) in enumerate(mla_scores):
        vvt = kvt_c[MLA_HEADS * LANES + hd * LANES:MLA_HEADS * LANES + (hd + 1) * LANES].astype(BF16)
        o = _softmax_pv(s_c, s_n, vvt, vm2_ref[0, :, hd * PAIR:hd * PAIR + LANES])
        mla_ref[0, :, hd * LANES:(hd + 1) * LANES] = o.astype(mla_ref.dtype)


def _sample_attention(l, kct, vct, kbct, ckvc, krct, wukvt, qf2, kf2, vf2, qm2, km2, vm2, *, chunk):
    b, n = qf2.shape[:2]
    past = kct.shape[3]
    cache = lambda a: pl.BlockSpec((1, 1) + a.shape[2:], lambda i: (l, i, 0, 0))
    per_b = lambda a: pl.BlockSpec((1,) + a.shape[1:], lambda i: (i, 0, 0))
    return pl.pallas_call(
        functools.partial(_sample_attn_kernel, past=past, chunk=chunk),
        grid=(b,),
        in_specs=[cache(kct), cache(vct), cache(kbct), cache(ckvc), cache(krct),
                  pl.BlockSpec(wukvt.shape, lambda i: (0, 0)),
                  per_b(qf2), per_b(kf2), per_b(vf2), per_b(qm2), per_b(km2), per_b(vm2)],
        out_specs=(pl.BlockSpec((1, n, FOX_WIDTH), lambda i: (i, 0, 0)),
                   pl.BlockSpec((1, n, MLA_WIDTH), lambda i: (i, 0, 0))),
        out_shape=(jax.ShapeDtypeStruct((b, n, FOX_WIDTH), BF16), jax.ShapeDtypeStruct((b, n, MLA_WIDTH), BF16)),
        compiler_params=_cparams(("parallel",)),
        name="sample_attn",
    )(kct, vct, kbct, ckvc, krct, wukvt, qf2, kf2, vf2, qm2, km2, vm2)


def _memkv_kernel(mem_ref, g_ref, wk_ref, wv_ref, mk_ref, mv_ref, mkb_ref, mvb_ref):
    m = _rms(mem_ref[0], g_ref[0]).astype(BF16)
    mk = _dot(m, wk_ref[0])
    mv = _dot(m, wv_ref[0])
    mk_ref[0, 0] = mk
    mv_ref[0, 0] = mv
    mkb_ref[0, 0] = mk.astype(BF16)
    mvb_ref[0, 0] = mv.astype(BF16)


def _memory_kv(mem, g, wk, wv):
    b, n, d = mem.shape
    depth = g.shape[0]
    o_spec = pl.BlockSpec((1, 1, n, d), lambda l, i: (l, i, 0, 0))
    w_spec = pl.BlockSpec((1, d, d), lambda l, i: (l, 0, 0))
    return pl.pallas_call(
        _memkv_kernel,
        grid=(depth, b),
        in_specs=[pl.BlockSpec((1, n, d), lambda l, i: (i, 0, 0)),
                  pl.BlockSpec((1, 1, d), lambda l, i: (l, 0, 0)), w_spec, w_spec],
        out_specs=(o_spec,) * 4,
        out_shape=(jax.ShapeDtypeStruct((depth, b, n, d), F32),) * 2 + (jax.ShapeDtypeStruct((depth, b, n, d), BF16),) * 2,
        compiler_params=_cparams(("arbitrary", "arbitrary")),
        name="memory_kv",
    )(mem, g, wk, wv)


def _post_kernel(x_ref, fox_ref, mla_ref, wo_ref, g_ref, wxq_ref, mk_ref, mv_ref, wxo_ref, o_ref, *, nb):
    x = x_ref[...] + _dot(fox_ref[...], wo_ref[:FOX_WIDTH, :]) + _dot(mla_ref[...], wo_ref[FOX_WIDTH:, :])
    hc = _rms(x, g_ref[...]).astype(BF16)
    q = (_dot(hc, wxq_ref[...]) * (X_HEAD_DIM ** -0.5)).astype(BF16)
    tm = x.shape[0] // nb
    rows = []
    for bi in range(nb):
        outs = []
        for hd in range(X_HEADS):
            sl = slice(hd * X_HEAD_DIM, (hd + 1) * X_HEAD_DIM)
            s = _dot_nt(q[bi * tm:(bi + 1) * tm, sl], mk_ref[0, bi, :, sl].astype(BF16))
            p = jnp.exp(s - jnp.max(s, axis=1, keepdims=True))
            den = jnp.sum(p, axis=1, keepdims=True)
            outs.append((_dot(p.astype(BF16), mv_ref[0, bi, :, sl].astype(BF16)) / den).astype(BF16))
        rows.append(jnp.concatenate(outs, axis=1))
    o = rows[0] if nb == 1 else jnp.concatenate(rows, axis=0)
    o_ref[...] = x + _dot(o, wxo_ref[...])


def _post(l, x, fox, mla, wo, g, wxq, mk, mv, wxo, *, b, tm, nb):
    n, d = x.shape
    s = n // b
    tiles = s // tm
    rows = nb * tm
    tok = lambda w: pl.BlockSpec((rows, w), lambda i, j: (i * tiles + j, 0))
    full = lambda a: pl.BlockSpec(a.shape, lambda i, j: (0,) * a.ndim)
    mem = pl.BlockSpec((1, nb) + mk.shape[2:], lambda i, j: (l, i) + (0,) * (mk.ndim - 2))
    return pl.pallas_call(
        functools.partial(_post_kernel, nb=nb),
        grid=(b // nb, tiles),
        in_specs=[tok(d), tok(FOX_WIDTH), tok(MLA_WIDTH), full(wo), full(g), full(wxq), mem, mem, full(wxo)],
        out_specs=tok(d),
        out_shape=jax.ShapeDtypeStruct((n, d), F32),
        compiler_params=_cparams(("parallel", "arbitrary")),
        name="post",
    )(x, fox, mla, wo, g, wxq, mk, mv, wxo)


def _mlp_kernel(x_ref, g_ref, wup_ref, wdn_ref, gf_ref, o_ref, *, final, ff_blk):
    x = x_ref[...]
    hm = _rms(x, g_ref[...]).astype(BF16)
    acc = x
    for c in range(D_FF // ff_blk):
        u = jnp.maximum(_dot(hm, wup_ref[:, c * ff_blk:(c + 1) * ff_blk]), 0.0)
        acc = acc + _dot(jnp.square(u).astype(BF16), wdn_ref[c * ff_blk:(c + 1) * ff_blk, :])
    o_ref[...] = _rms(acc, gf_ref[...]) if final else acc


def _mlp(x, g, wup, wdn, gf, *, final, tm):
    n, d = x.shape
    full = lambda a: pl.BlockSpec(a.shape, lambda i: (0,) * a.ndim, pipeline_mode=pl.Buffered(1))
    return pl.pallas_call(
        functools.partial(_mlp_kernel, final=final, ff_blk=1024),
        grid=(n // tm,),
        in_specs=[pl.BlockSpec((tm, d), lambda i: (i, 0)), full(g), full(wup), full(wdn), full(gf)],
        out_specs=pl.BlockSpec((tm, d), lambda i: (i, 0)),
        out_shape=jax.ShapeDtypeStruct((n, d), F32),
        compiler_params=_cparams(("parallel",)),
        name="mlp",
    )(x, g, wup, wdn, gf)


def _gate_block(a):
    pad = jnp.zeros(a.shape[:-1] + (LANES - FOX_HEADS * BIAS_SLOTS - FOX_HEADS,), a.dtype)
    return jnp.concatenate([jnp.repeat(a, BIAS_SLOTS, axis=-1), a, pad], axis=-1)


def _split_win(w):
    o = 3 * FOX_WIDTH
    o2 = o + FOX_HEADS + MLA_Q_RANK
    half = MLA_ROPE_DIM // 2
    kr = w[:, o2 + MLA_KV_RANK:]
    return dict(q=w[:, :FOX_WIDTH], k=w[:, FOX_WIDTH:2 * FOX_WIDTH], v=w[:, 2 * FOX_WIDTH:o], g=w[:, o:o + FOX_HEADS],
                cq=w[:, o + FOX_HEADS:o2], ckv=w[:, o2:o2 + MLA_KV_RANK], kr=kr,
                kr_sw=jnp.concatenate([kr[:, half:], kr[:, :half]], axis=1))


def _layout_win_sample(w):
    p = _split_win(w)
    return jnp.concatenate([p["q"], p["k"], p["v"], p["cq"], p["ckv"], p["kr"], p["kr_sw"], _gate_block(p["g"])], axis=1).astype(BF16)


def _layout_win_prompt(w):
    p = _split_win(w)
    wr = jnp.concatenate([p["k"], p["ckv"], p["kr"], p["kr_sw"], _gate_block(p["g"])], axis=1).astype(BF16)
    wt = jnp.concatenate([p["q"], p["v"], p["cq"], p["ckv"], p["kr"]], axis=1).T.astype(BF16)
    return wr, wt


def _layout_wuq_sample(w):
    w = w.reshape(MLA_Q_RANK, MLA_HEADS, MLA_NOPE_DIM + MLA_ROPE_DIM)
    nope = w[:, :, :MLA_NOPE_DIM].reshape(MLA_Q_RANK, MLA_HEADS * MLA_NOPE_DIM)
    r = w[:, :, MLA_NOPE_DIM:]
    half = MLA_ROPE_DIM // 2
    r2 = jnp.concatenate([r, r[:, :, half:], r[:, :, :half]], axis=2).reshape(MLA_Q_RANK, MLA_HEADS * LANES)
    return jnp.concatenate([nope, r2], axis=1).astype(BF16)


def _layout_wuq_prompt(w):
    w = w.reshape(MLA_Q_RANK, MLA_HEADS, MLA_NOPE_DIM + MLA_ROPE_DIM)
    return jnp.concatenate([w[:, :, :MLA_NOPE_DIM].reshape(MLA_Q_RANK, -1),
                            w[:, :, MLA_NOPE_DIM:].reshape(MLA_Q_RANK, -1)], axis=1).T.astype(BF16)


def _layout_wukv(w):
    w = w.reshape(MLA_KV_RANK, MLA_HEADS, MLA_NOPE_DIM + MLA_V_DIM)
    return jnp.concatenate([w[:, :, :MLA_NOPE_DIM].reshape(MLA_KV_RANK, -1),
                            w[:, :, MLA_NOPE_DIM:].reshape(MLA_KV_RANK, -1)], axis=1).astype(BF16)


def _rope_tables(pos):
    half = MLA_ROPE_DIM // 2
    inv = ROPE_THETA ** (-jnp.arange(half, dtype=F32) / half)
    ang = pos.astype(F32)[:, None] * inv[None, :]
    return jnp.cos(ang), jnp.sin(ang)


def _tile(n, pref):
    return pref if n % pref == 0 else n


def kernel(x_prompt, x_sample, mem_prompt, cache_fox_k, cache_fox_v, cache_fox_logf, cache_mla_ckv, cache_mla_krope,
           cache_mem_k, cache_mem_v, norm_mix, w_in, b_forget, mla_q_norm, w_uq, mla_kv_norm, w_ukv, w_out,
           norm_cross, norm_mem, w_xq, w_mk, w_mv, w_xo, norm_mlp, w_up, w_down, norm_final):
    depth = w_in.shape[0]
    bp, sp, d = x_prompt.shape
    bs, ss, _ = x_sample.shape
    past = cache_fox_k.shape[2]
    n_mem = mem_prompt.shape[1]
    row = lambda a: a.reshape(1, -1)
    col = lambda a: a.reshape(-1, 1)

    cos_p, sin_p = _rope_tables(jnp.arange(sp))
    cos_s, sin_s = _rope_tables(past + jnp.arange(ss))
    cs_p = jnp.concatenate([cos_p, cos_p, -sin_p, sin_p], axis=1)
    cs_s = jnp.tile(jnp.concatenate([cos_s, cos_s, -sin_s, sin_s], axis=1), (bs, 1))
    cos_pt, sin_pt = cos_p.T, sin_p.T

    mk_all, mv_all, mkb_all, mvb_all = _memory_kv(mem_prompt, norm_mem[:, None, :], w_mk.astype(BF16), w_mv.astype(BF16))

    kct = jnp.transpose(cache_fox_k, (0, 1, 3, 4, 2)).reshape(depth, bs, FOX_WIDTH, past)
    vct = jnp.transpose(cache_fox_v, (0, 1, 3, 4, 2)).reshape(depth, bs, FOX_WIDTH, past)
    krct = jnp.transpose(cache_mla_krope, (0, 1, 3, 2))
    kbct, tot = _cache_scan(jnp.transpose(cache_fox_logf, (0, 1, 3, 2)).astype(F32))
    cin_all = _gate_block(tot[..., 0])[:, :, None, :]

    cmk = cache_mem_k.reshape(depth, bs, n_mem, d)
    cmv = cache_mem_v.reshape(depth, bs, n_mem, d)

    tm_p = _tile(sp, 512)
    xp, xs = x_prompt, x_sample
    kt_all = vt_all = None
    rows_p, rows_s = [], []
    for l in range(depth):
        wsplit_s = _layout_win_sample(w_in[l])
        wr, wt = _layout_win_prompt(w_in[l])
        bblk = _gate_block(b_forget[l][None, :])
        wukv = _layout_wukv(w_ukv[l])
        wnope, wvt = wukv[:, :MLA_HEADS * MLA_NOPE_DIM], wukv[:, MLA_HEADS * MLA_NOPE_DIM:].T
        wo, wxq, wxo = w_out[l].astype(BF16), w_xq[l].astype(BF16), w_xo[l].astype(BF16)
        wup, wdn = w_up[l].astype(BF16), w_down[l].astype(BF16)
        gmix, gq, gkv, gx, gm = row(norm_mix[l]), row(mla_q_norm[l]), row(mla_kv_norm[l]), row(norm_cross[l]), row(norm_mlp[l])
        final = l == depth - 1

        kt_all, vt_all, lft, ckv, krt, qt, k2, vt2, qmt, km2, vmt = _inproj_prompt(
            xp, cs_p, cos_pt, sin_pt, gmix, wr, wt, bblk, col(mla_q_norm[l]), _layout_wuq_prompt(w_uq[l]),
            gkv, col(mla_kv_norm[l]), wnope, wvt, kt_all, vt_all, tm=tm_p)
        fox = _attention(qt, k2, vt2, n_chain=4, n_sub=2, chunk=1, v_rows=FOX_V_ROWS, v_dim=FOX_HEAD_DIM)
        mla = _attention(qmt, km2, vmt, n_chain=4, n_sub=1, chunk=CHUNK, v_rows=MLA_V_ROWS, v_dim=MLA_V_DIM)
        xp = _post(l, xp.reshape(bp * sp, d), fox.reshape(bp * sp, FOX_WIDTH), mla.reshape(bp * sp, MLA_WIDTH),
                   wo, gx, wxq, mkb_all, mvb_all, wxo, b=bp, tm=tm_p, nb=1)
        xp = _mlp(xp, gm, wup, wdn, row(norm_final), final=final, tm=_tile(bp * sp, 512)).reshape(bp, sp, d)
        rows_p.append((lft, ckv, krt))

        ns = bs * ss
        outs = _inproj(xs.reshape(1, ns, d), cin_all[l], cs_s, gmix, wsplit_s, bblk, gq, _layout_wuq_sample(w_uq[l]),
                       gkv, wukv, tm=_tile(ns, 8 * ss), seg=ss)
        kf, vf, lf, ckv, kr, qf2, kf2, vf2, qm2, km2, vm2 = (o.reshape(bs, ss, o.shape[-1]) for o in outs)
        fox, mla = _sample_attention(l, kct, vct, kbct, cache_mla_ckv, krct, wukv.T, qf2, kf2, vf2, qm2, km2, vm2, chunk=CHUNK)
        xs = _post(l, xs.reshape(ns, d), fox.reshape(ns, FOX_WIDTH), mla.reshape(ns, MLA_WIDTH),
                   wo, gx, wxq, cmk, cmv, wxo, b=bs, tm=ss, nb=4 if bs % 4 == 0 else 1)
        xs = _mlp(xs, gm, wup, wdn, row(norm_final), final=final, tm=_tile(ns, 512)).reshape(bs, ss, d)
        rows_s.append((kf, vf, lf, ckv, kr))

    st = lambda rows, i: jnp.stack([r[i] for r in rows])
    heads5 = lambda a: jnp.transpose(a.reshape(depth, bp, FOX_HEADS, FOX_HEAD_DIM, sp), (0, 1, 4, 2, 3))
    lo = FOX_HEADS * BIAS_SLOTS
    mem_shape = (depth, bp, n_mem, X_HEADS, X_HEAD_DIM)
    return (xp, xs,
            heads5(kt_all), heads5(vt_all), jnp.transpose(st(rows_p, 0), (0, 1, 3, 2)), st(rows_p, 1),
            jnp.transpose(st(rows_p, 2), (0, 1, 3, 2)),
            mk_all.reshape(mem_shape), mv_all.reshape(mem_shape),
            st(rows_s, 0).reshape(depth, bs, ss, FOX_HEADS, FOX_HEAD_DIM), st(rows_s, 1).reshape(depth, bs, ss, FOX_HEADS, FOX_HEAD_DIM),
            st(rows_s, 2)[..., lo:lo + FOX_HEADS], st(rows_s, 3), st(rows_s, 4))
```

```python
import functools

import jax
import jax.numpy as jnp
from jax import lax
from jax.experimental import pallas as pl
from jax.experimental.pallas import tpu as pltpu

F32 = jnp.float32
BF16 = jnp.bfloat16

D_MODEL = 1024
CHUNK = 64
FOX_HEADS = 8
FOX_HEAD_DIM = 64
FOX_WIDTH = FOX_HEADS * FOX_HEAD_DIM
MLA_HEADS = 4
MLA_NOPE_DIM = 128
MLA_ROPE_DIM = 64
MLA_V_DIM = 128
MLA_WIDTH = MLA_HEADS * MLA_V_DIM
MLA_Q_RANK = 256
MLA_KV_RANK = 128
ROPE_THETA = 10000.0
X_HEADS = 4
X_HEAD_DIM = 256
D_FF = 4 * D_MODEL
EPS = 1e-6

LOG2E = 1.4426950408889634
LANES = 128
SUBLANES = 8
BF16_ROWS = 16
PAIR = 2 * LANES
BIAS_SLOTS = 8
VMEM_LIMIT = 56 * 1024 * 1024
FOX_V_ROWS = FOX_HEAD_DIM + BF16_ROWS
MLA_V_ROWS = MLA_V_DIM + BF16_ROWS

_OQ, _OK, _OV = 0, FOX_WIDTH, 2 * FOX_WIDTH
_OCQ = 3 * FOX_WIDTH
_OCKV = _OCQ + MLA_Q_RANK
_OKR = _OCKV + MLA_KV_RANK
_OG = _OKR + LANES
_RK, _RCKV, _RKR, _RG = 0, FOX_WIDTH, FOX_WIDTH + MLA_KV_RANK, FOX_WIDTH + MLA_KV_RANK + LANES
_TQ, _TV = 0, FOX_WIDTH
_TCQ = 2 * FOX_WIDTH
_TCKV = _TCQ + MLA_Q_RANK
_TKR = _TCKV + MLA_KV_RANK


def _cparams(sem):
    return pltpu.CompilerParams(dimension_semantics=sem, vmem_limit_bytes=VMEM_LIMIT)


def _rms(x, g):
    return x * lax.rsqrt(jnp.mean(jnp.square(x), axis=-1, keepdims=True) + EPS) * g


def _rms_rows(x, g):
    return x * lax.rsqrt(jnp.mean(jnp.square(x), axis=0, keepdims=True) + EPS) * g


def _dot(a, b):
    return jnp.dot(a, b, preferred_element_type=F32)


def _dot_nt(a, b):
    return lax.dot_general(a, b, (((1,), (1,)), ((), ())), preferred_element_type=F32)


def _split3(x):
    x1 = x.astype(BF16)
    r1 = x - x1.astype(F32)
    x2 = r1.astype(BF16)
    r2 = r1 - x2.astype(F32)
    return x1, x2, r2.astype(BF16)


def _cumsum_rows(x, carry):
    n = x.shape[0]
    blk = min(n, 256)
    r = lax.broadcasted_iota(jnp.int32, (blk, blk), 0)
    c = lax.broadcasted_iota(jnp.int32, (blk, blk), 1)
    tri = (c <= r).astype(BF16)
    outs = []
    for i in range(n // blk):
        x1, x2, x3 = _split3(x[i * blk:(i + 1) * blk])
        cb = _dot(tri, x1) + _dot(tri, x2) + _dot(tri, x3) + carry
        carry = cb[blk - 1:blk]
        outs.append(cb)
    return outs[0] if len(outs) == 1 else jnp.concatenate(outs, axis=0)


def _cumsum_lanes(x):
    n = x.shape[1]
    blk = min(n, 256)
    r = lax.broadcasted_iota(jnp.int32, (blk, blk), 0)
    c = lax.broadcasted_iota(jnp.int32, (blk, blk), 1)
    tri = (r <= c).astype(BF16)
    carry = jnp.zeros((x.shape[0], 1), F32)
    outs = []
    for i in range(n // blk):
        x1, x2, x3 = _split3(x[:, i * blk:(i + 1) * blk])
        cb = _dot(x1, tri) + _dot(x2, tri) + _dot(x3, tri) + carry
        carry = cb[:, blk - 1:blk]
        outs.append(cb)
    return outs[0] if len(outs) == 1 else jnp.concatenate(outs, axis=1)


def _bias_blocks(cum, slot_axis):
    c1, c2, c3 = (c.astype(F32) for c in _split3(cum))
    slot = lax.broadcasted_iota(jnp.int32, cum.shape, slot_axis)
    j = slot % BIAS_SLOTS
    valid = slot < FOX_HEADS * BIAS_SLOTS
    one = jnp.where(valid & (j < 6), 1.0, 0.0)
    qb = jnp.where(j == 0, c1, jnp.where(j == 1, c2, jnp.where(j == 2, c3, one)))
    kb = jnp.where(j < 3, one, jnp.where(j == 3, -c1, jnp.where(j == 4, -c2, jnp.where(j == 5, -c3, 0.0))))
    return jnp.where(valid, qb, 0.0), jnp.where(valid, kb, 0.0)


def _log_sigmoid(x):
    return jnp.minimum(x, 0.0) - jnp.log1p(jnp.exp(-jnp.abs(x)))


def _rope_pair(t, cs):
    u = t * cs
    return u + pltpu.roll(u, MLA_ROPE_DIM, 1)


def _rope_rows(x, cos, sin):
    half = MLA_ROPE_DIM // 2
    x1, x2 = x[:half], x[half:]
    return jnp.concatenate([x1 * cos - x2 * sin, x2 * cos + x1 * sin], axis=0)


def _inproj_kernel(x_ref, cin_ref, cs_ref, gmix_ref, win_ref, bblk_ref, gq_ref, wuq_ref, gkv_ref, wukv_ref,
                   kf_ref, vf_ref, lf_ref, ckv_ref, kr_ref,
                   qf2_ref, kf2_ref, vf2_ref, qm2_ref, km2_ref, vm2_ref, *, seg):
    x = x_ref[0]
    tm = x.shape[0]
    h = _rms(x, gmix_ref[...])
    z = _dot(h.astype(BF16), win_ref[...])
    cs = cs_ref[...]
    lane = lax.broadcasted_iota(jnp.int32, (1, LANES), 1)
    low = lane < MLA_ROPE_DIM
    ones_blk = jnp.ones((tm, LANES), BF16)

    kf_ref[0] = z[:, _OK:_OK + FOX_WIDTH]
    vf_ref[0] = z[:, _OV:_OV + FOX_WIDTH]

    lf = _log_sigmoid(z[:, _OG:_OG + LANES] + bblk_ref[...])
    lf_ref[0] = lf
    cums = [_cumsum_rows(lf[i * seg:(i + 1) * seg], cin_ref[i]) for i in range(tm // seg)]
    cum = cums[0] if len(cums) == 1 else jnp.concatenate(cums, axis=0)
    qb, kb = _bias_blocks(cum, 1)

    fox_scale = FOX_HEAD_DIM ** -0.5
    for p in range(FOX_HEADS // 2):
        lo, hi = p * PAIR, p * PAIR + LANES
        qf2_ref[0, :, lo:hi] = (z[:, _OQ + p * LANES:_OQ + (p + 1) * LANES] * fox_scale).astype(BF16)
        qf2_ref[0, :, hi:hi + LANES] = qb.astype(BF16)
        kf2_ref[0, :, lo:hi] = z[:, _OK + p * LANES:_OK + (p + 1) * LANES].astype(BF16)
        kf2_ref[0, :, hi:hi + LANES] = kb.astype(BF16)
        vf2_ref[0, :, lo:hi] = z[:, _OV + p * LANES:_OV + (p + 1) * LANES].astype(BF16)
        vf2_ref[0, :, hi:hi + LANES] = ones_blk

    kr = _rope_pair(z[:, _OKR:_OKR + LANES], cs)
    kr_ref[0] = kr[:, :MLA_ROPE_DIM]
    krb = jnp.where(low, kr, 0.0).astype(BF16)

    ckv = _rms(z[:, _OCKV:_OCKV + MLA_KV_RANK], gkv_ref[...])
    ckv_ref[0] = ckv
    kv = _dot(ckv.astype(BF16), wukv_ref[...])
    cqn = _rms(z[:, _OCQ:_OCQ + MLA_Q_RANK], gq_ref[...])
    qm = _dot(cqn.astype(BF16), wuq_ref[...])
    mla_scale = (MLA_NOPE_DIM + MLA_ROPE_DIM) ** -0.5
    for hd in range(MLA_HEADS):
        lo, hi = hd * PAIR, hd * PAIR + LANES
        qr = _rope_pair(qm[:, MLA_HEADS * LANES + hd * LANES:MLA_HEADS * LANES + (hd + 1) * LANES], cs)
        qm2_ref[0, :, lo:hi] = (qm[:, hd * LANES:(hd + 1) * LANES] * mla_scale).astype(BF16)
        qm2_ref[0, :, hi:hi + LANES] = (jnp.where(low, qr, 0.0) * mla_scale).astype(BF16)
        km2_ref[0, :, lo:hi] = kv[:, hd * LANES:(hd + 1) * LANES].astype(BF16)
        km2_ref[0, :, hi:hi + LANES] = krb
        vm2_ref[0, :, lo:hi] = kv[:, MLA_HEADS * LANES + hd * LANES:MLA_HEADS * LANES + (hd + 1) * LANES].astype(BF16)
        vm2_ref[0, :, hi:hi + LANES] = ones_blk


def _inproj(x, cin, cs, gmix, win, bblk, gq, wuq, gkv, wukv, *, tm, seg):
    b, s, _ = x.shape
    grid = (b, s // tm)
    tok = lambda w: pl.BlockSpec((1, tm, w), lambda i, j: (i, j, 0))
    full = lambda a: pl.BlockSpec(a.shape, lambda i, j: (0,) * a.ndim)
    out_shape = (
        jax.ShapeDtypeStruct((b, s, FOX_WIDTH), F32), jax.ShapeDtypeStruct((b, s, FOX_WIDTH), F32),
        jax.ShapeDtypeStruct((b, s, LANES), F32), jax.ShapeDtypeStruct((b, s, MLA_KV_RANK), F32),
        jax.ShapeDtypeStruct((b, s, MLA_ROPE_DIM), F32),
    ) + tuple(jax.ShapeDtypeStruct((b, s, 4 * PAIR), BF16) for _ in range(6))
    out_specs = (tok(FOX_WIDTH), tok(FOX_WIDTH), tok(LANES), tok(MLA_KV_RANK), tok(MLA_ROPE_DIM)) + (tok(4 * PAIR),) * 6
    return pl.pallas_call(
        functools.partial(_inproj_kernel, seg=seg),
        grid=grid,
        in_specs=[tok(D_MODEL), pl.BlockSpec((tm // seg, 1, LANES), lambda i, j: (j, 0, 0)),
                  pl.BlockSpec((tm, LANES), lambda i, j: (j, 0)),
                  full(gmix), full(win), full(bblk), full(gq), full(wuq), full(gkv), full(wukv)],
        out_specs=out_specs,
        out_shape=out_shape,
        compiler_params=_cparams(("parallel", "parallel")),
        name="inproj_sample",
    )(x, cin, cs, gmix, win, bblk, gq, wuq, gkv, wukv)


def _inproj_prompt_kernel(*refs, n_prev):
    (x_ref, cs_ref, cos_ref, sin_ref, gmix_ref, wr_ref, wt_ref, bblk_ref, gq_ref, wuqt_ref, gkv_ref, gkvc_ref,
     wnope_ref, wvt_ref) = refs[:14]
    prev_refs = refs[14:19] if n_prev else ()
    (kt_ref, vt_ref, lft_ref, ckv_ref, krt_ref,
     qt_ref, k2_ref, vt2_ref, qmt_ref, km2_ref, vmt_ref, carry_ref) = refs[14 + len(prev_refs):]

    @pl.when(pl.program_id(1) == 0)
    def _():
        carry_ref[...] = jnp.zeros_like(carry_ref)

    x = x_ref[0]
    tm = x.shape[0]
    hb = _rms(x, gmix_ref[...]).astype(BF16)
    zr = _dot(hb, wr_ref[...])
    zt = _dot_nt(wt_ref[...], hb)
    cs, cos, sin = cs_ref[...], cos_ref[...], sin_ref[...]
    ones_rows = jnp.ones((BF16_ROWS, tm), BF16)

    for prev_ref, out_ref in zip(prev_refs, (kt_ref, vt_ref, lft_ref, ckv_ref, krt_ref)):
        out_ref[:n_prev, 0] = prev_ref[:, 0]
    kt_ref[n_prev, 0] = zr[:, _RK:_RK + FOX_WIDTH].T
    vt_ref[n_prev, 0] = zt[_TV:_TV + FOX_WIDTH]

    lf = _log_sigmoid(zr[:, _RG:_RG + LANES] + bblk_ref[...])
    cum = _cumsum_rows(lf, carry_ref[...])
    carry_ref[...] = cum[tm - 1:tm]
    qb, kb = _bias_blocks(cum * LOG2E, 1)
    kb = kb.astype(BF16)
    qbt = qb.T
    lo = FOX_HEADS * BIAS_SLOTS
    lft_ref[n_prev, 0] = lf.T[lo:lo + FOX_HEADS]

    fox_scale = FOX_HEAD_DIM ** -0.5 * LOG2E
    rowi = lax.broadcasted_iota(jnp.int32, (LANES, tm), 0)
    for p in range(FOX_HEADS // 2):
        zq = zt[_TQ + p * LANES:_TQ + (p + 1) * LANES] * fox_scale
        for sub in range(2):
            hd = 2 * p + sub
            qt_ref[0, 0, hd * PAIR:hd * PAIR + LANES] = jnp.where(rowi // FOX_HEAD_DIM == sub, zq, 0.0).astype(BF16)
            qt_ref[0, 0, hd * PAIR + LANES:(hd + 1) * PAIR] = jnp.where(rowi // BIAS_SLOTS == hd, qbt, 0.0).astype(BF16)
        k2_ref[0, :, p * PAIR:p * PAIR + LANES] = zr[:, _RK + p * LANES:_RK + (p + 1) * LANES].astype(BF16)
        k2_ref[0, :, p * PAIR + LANES:(p + 1) * PAIR] = kb
    for hd in range(FOX_HEADS):
        r0 = hd * FOX_V_ROWS
        vt2_ref[0, 0, r0:r0 + FOX_HEAD_DIM] = zt[_TV + hd * FOX_HEAD_DIM:_TV + (hd + 1) * FOX_HEAD_DIM].astype(BF16)
        vt2_ref[0, 0, r0 + FOX_HEAD_DIM:r0 + FOX_V_ROWS] = ones_rows

    krt_ref[n_prev, 0] = _rope_rows(zt[_TKR:_TKR + MLA_ROPE_DIM], cos, sin)
    kr = _rope_pair(zr[:, _RKR:_RKR + LANES], cs)
    krb = jnp.where(lax.broadcasted_iota(jnp.int32, (1, LANES), 1) < MLA_ROPE_DIM, kr, 0.0).astype(BF16)
    ckv = _rms(zr[:, _RCKV:_RCKV + MLA_KV_RANK], gkv_ref[...])
    ckv_ref[n_prev, 0] = ckv
    knope = _dot(ckv.astype(BF16), wnope_ref[...])
    ckvt = _rms_rows(zt[_TCKV:_TCKV + MLA_KV_RANK], gkvc_ref[...])
    vmt = _dot(wvt_ref[...], ckvt.astype(BF16))
    cqnt = _rms_rows(zt[_TCQ:_TCQ + MLA_Q_RANK], gq_ref[...])
    qmt = _dot(wuqt_ref[...], cqnt.astype(BF16))
    mla_scale = (MLA_NOPE_DIM + MLA_ROPE_DIM) ** -0.5 * LOG2E
    zeros_rows = jnp.zeros((PAIR - MLA_NOPE_DIM - MLA_ROPE_DIM, tm), BF16)
    for hd in range(MLA_HEADS):
        q0 = hd * PAIR
        qmt_ref[0, 0, q0:q0 + MLA_NOPE_DIM] = (qmt[hd * MLA_NOPE_DIM:(hd + 1) * MLA_NOPE_DIM] * mla_scale).astype(BF16)
        r0 = MLA_HEADS * MLA_NOPE_DIM + hd * MLA_ROPE_DIM
        qmt_ref[0, 0, q0 + MLA_NOPE_DIM:q0 + MLA_NOPE_DIM + MLA_ROPE_DIM] = (
            _rope_rows(qmt[r0:r0 + MLA_ROPE_DIM], cos, sin) * mla_scale).astype(BF16)
        qmt_ref[0, 0, q0 + MLA_NOPE_DIM + MLA_ROPE_DIM:q0 + PAIR] = zeros_rows
        km2_ref[0, :, q0:q0 + LANES] = knope[:, hd * LANES:(hd + 1) * LANES].astype(BF16)
        km2_ref[0, :, q0 + LANES:q0 + PAIR] = krb
        v0 = hd * MLA_V_ROWS
        vmt_ref[0, 0, v0:v0 + MLA_V_DIM] = vmt[hd * MLA_V_DIM:(hd + 1) * MLA_V_DIM].astype(BF16)
        vmt_ref[0, 0, v0 + MLA_V_DIM:v0 + MLA_V_ROWS] = ones_rows


def _inproj_prompt(x, cs, cos_t, sin_t, gmix, wr, wt, bblk, gq_col, wuqt, gkv, gkv_col, wnope, wvt, prev, *, tm):
    b, s, _ = x.shape
    nt = s // tm
    n_prev = 0 if prev is None else prev[0].shape[0]
    full = lambda a: pl.BlockSpec(a.shape, lambda i, j: (0,) * a.ndim, pipeline_mode=pl.Buffered(1))
    tile4 = lambda rows: pl.BlockSpec((1, 1, rows, tm), lambda i, j: (i, j, 0, 0))
    tok = lambda w: pl.BlockSpec((1, tm, w), lambda i, j: (i, j, 0))
    feat = lambda n, rows: pl.BlockSpec((n, 1, rows, tm), lambda i, j: (0, i, 0, j))
    toks = lambda n, w: pl.BlockSpec((n, 1, tm, w), lambda i, j: (0, i, j, 0))
    row_specs = lambda n: [feat(n, FOX_WIDTH), feat(n, FOX_WIDTH), feat(n, FOX_HEADS), toks(n, MLA_KV_RANK),
                           feat(n, MLA_ROPE_DIM)]
    ins = [x, cs, cos_t, sin_t, gmix, wr, wt, bblk, gq_col, wuqt, gkv, gkv_col, wnope, wvt]
    in_specs = [tok(D_MODEL), pl.BlockSpec((tm, LANES), lambda i, j: (j, 0)),
                pl.BlockSpec((cos_t.shape[0], tm), lambda i, j: (0, j)),
                pl.BlockSpec((sin_t.shape[0], tm), lambda i, j: (0, j))] + [full(a) for a in ins[4:]]
    if n_prev:
        ins += list(prev)
        in_specs += row_specs(n_prev)
    op4 = lambda rows: jax.ShapeDtypeStruct((b, nt, rows, tm), BF16)
    nl = n_prev + 1
    out_shape = (jax.ShapeDtypeStruct((nl, b, FOX_WIDTH, s), F32),) * 2 + (
        jax.ShapeDtypeStruct((nl, b, FOX_HEADS, s), F32), jax.ShapeDtypeStruct((nl, b, s, MLA_KV_RANK), F32),
        jax.ShapeDtypeStruct((nl, b, MLA_ROPE_DIM, s), F32),
        op4(FOX_HEADS * PAIR), jax.ShapeDtypeStruct((b, s, 4 * PAIR), BF16), op4(FOX_HEADS * FOX_V_ROWS),
        op4(MLA_HEADS * PAIR), jax.ShapeDtypeStruct((b, s, 4 * PAIR), BF16), op4(MLA_HEADS * MLA_V_ROWS))
    out_specs = tuple(row_specs(nl)) + (
        tile4(FOX_HEADS * PAIR), tok(4 * PAIR), tile4(FOX_HEADS * FOX_V_ROWS),
        tile4(MLA_HEADS * PAIR), tok(4 * PAIR), tile4(MLA_HEADS * MLA_V_ROWS))
    return pl.pallas_call(
        functools.partial(_inproj_prompt_kernel, n_prev=n_prev),
        grid=(b, nt),
        in_specs=in_specs,
        out_specs=out_specs,
        out_shape=out_shape,
        scratch_shapes=[pltpu.VMEM((1, LANES), F32)],
        compiler_params=_cparams(("parallel", "arbitrary")),
        name="inproj_prompt",
    )(*ins)


def _attn_kernel(qt_ref, k_ref, vt_ref, o_ref, sa_ref, sb_ref, sd_ref, m_ref, acc_ref, *,
                 t, nt, n_chain, n_sub, chunk, v_rows, v_dim):
    i = pl.program_id(2)
    key = lax.broadcasted_iota(jnp.int32, (t, t), 0)
    qry = lax.broadcasted_iota(jnp.int32, (t, t), 1)
    allowed = (key // chunk) <= (qry // chunk)

    def scores(qblk, kblk, dst_ref, diagonal=False):
        off = pl.multiple_of(kblk * t, t)
        for c in range(n_chain):
            g = c // n_sub
            st = _dot(k_ref[0, pl.ds(off, t), g * PAIR:(g + 1) * PAIR], qt_ref[0, qblk, c * PAIR:(c + 1) * PAIR])
            if diagonal:
                st = jnp.where(allowed, st, -1e30)
            dst_ref[c, :t] = st
            dst_ref[c, t:] = jnp.broadcast_to(jnp.max(st, axis=0, keepdims=True), (SUBLANES, t))

    def consume(kblk, src_ref):
        for c in range(n_chain):
            st = src_ref[c, :t]
            m = m_ref[c]
            m_new = jnp.maximum(m, src_ref[c, t:t + 1])
            pt = jnp.exp2(st - m_new).astype(BF16)
            acc_ref[c] = jnp.exp2(m - m_new) * acc_ref[c] + _dot(vt_ref[0, kblk, c * v_rows:(c + 1) * v_rows], pt)
            m_ref[c] = m_new

    def next_diagonal():
        nxt = jnp.minimum(i + 1, nt - 1)
        scores(nxt, nxt, sd_ref, diagonal=True)

    @pl.when(i == 0)
    def _():
        scores(0, 0, sd_ref, diagonal=True)

    m_ref[...] = jnp.full(m_ref.shape, -1e30, F32)
    acc_ref[...] = jnp.zeros(acc_ref.shape, F32)

    @pl.when(i == 0)
    def _():
        consume(0, sd_ref)
        next_diagonal()

    @pl.when(i > 0)
    def _():
        scores(i, 0, sa_ref)
        consume(i, sd_ref)
        n_pairs = (i - 1) // 2

        def pair(p, carry):
            scores(i, 2 * p + 1, sb_ref)
            consume(2 * p, sa_ref)
            scores(i, 2 * p + 2, sa_ref)
            consume(2 * p + 1, sb_ref)
            return carry

        lax.fori_loop(0, n_pairs, pair, 0)
        one_more = (i - 1) % 2 == 1

        @pl.when(one_more)
        def _():
            scores(i, i - 1, sb_ref)
            consume(2 * n_pairs, sa_ref)
            next_diagonal()
            consume(i - 1, sb_ref)

        @pl.when(jnp.logical_not(one_more))
        def _():
            next_diagonal()
            consume(2 * n_pairs, sa_ref)

    for g in range(n_chain // n_sub):
        outs = []
        for sub in range(n_sub):
            acc = acc_ref[g * n_sub + sub]
            outs.append(acc[:v_dim] / acc[v_dim:v_dim + 1])
        ot = outs[0] if n_sub == 1 else jnp.concatenate(outs, axis=0)
        o_ref[0, :, g * LANES:(g + 1) * LANES] = ot.T.astype(o_ref.dtype)


def _attention(qt, k2, vt, *, n_chain, n_sub, chunk, v_rows, v_dim):
    b, nt, rows, t = qt.shape
    s = nt * t
    steps = rows // (n_chain * PAIR)
    gps = n_chain // n_sub
    return pl.pallas_call(
        functools.partial(_attn_kernel, t=t, nt=nt, n_chain=n_chain, n_sub=n_sub, chunk=chunk, v_rows=v_rows, v_dim=v_dim),
        grid=(b, steps, nt),
        in_specs=[pl.BlockSpec((1, nt, n_chain * PAIR, t), lambda bi, g, i: (bi, 0, g, 0)),
                  pl.BlockSpec((1, s, gps * PAIR), lambda bi, g, i: (bi, 0, g)),
                  pl.BlockSpec((1, nt, n_chain * v_rows, t), lambda bi, g, i: (bi, 0, g, 0),
                               pipeline_mode=pl.Buffered(1))],
        out_specs=pl.BlockSpec((1, t, gps * LANES), lambda bi, g, i: (bi, i, g)),
        out_shape=jax.ShapeDtypeStruct((b, s, 4 * LANES), BF16),
        scratch_shapes=[pltpu.VMEM((n_chain, t + SUBLANES, t), F32)] * 3
                       + [pltpu.VMEM((n_chain, 1, t), F32), pltpu.VMEM((n_chain, v_rows, t), F32)],
        compiler_params=_cparams(("parallel", "arbitrary", "arbitrary")),
        name="attn_fox" if n_sub > 1 else "attn_mla",
    )(qt, k2, vt)


def _cache_scan_kernel(lft_ref, kbct_ref, tot_ref):
    lf = lft_ref[0, 0]
    heads, n = lf.shape
    cum = _cumsum_lanes(jnp.concatenate([lf, jnp.zeros_like(lf)], axis=0))[:heads]
    tot_ref[0, 0] = jnp.broadcast_to(cum[:, n - 1:n], (heads, LANES))
    rep = jnp.concatenate([jnp.broadcast_to(cum[hd:hd + 1], (BIAS_SLOTS, n)) for hd in range(heads)]
                          + [jnp.zeros((LANES - heads * BIAS_SLOTS, n), F32)], axis=0)
    kbct_ref[0, 0] = _bias_blocks(rep, 0)[1].astype(BF16)


def _cache_scan(lft):
    depth, b, heads, n = lft.shape
    return pl.pallas_call(
        _cache_scan_kernel,
        grid=(depth, b),
        in_specs=[pl.BlockSpec((1, 1, heads, n), lambda l, i: (l, i, 0, 0))],
        out_specs=(pl.BlockSpec((1, 1, LANES, n), lambda l, i: (l, i, 0, 0)),
                   pl.BlockSpec((1, 1, heads, LANES), lambda l, i: (l, i, 0, 0))),
        out_shape=(jax.ShapeDtypeStruct((depth, b, LANES, n), BF16), jax.ShapeDtypeStruct((depth, b, heads, LANES), F32)),
        compiler_params=_cparams(("parallel", "parallel")),
        name="cache_scan",
    )(lft)


def _select_head(q2, g, sub):
    lane = lax.broadcasted_iota(jnp.int32, (1, PAIR), 1)
    payload = (lane < LANES) & (lane // FOX_HEAD_DIM == sub)
    side = (lane >= LANES) & ((lane - LANES) // BIAS_SLOTS == 2 * g + sub)
    return jnp.where(payload | side, q2.astype(F32), 0.0).astype(BF16)


def _softmax_pv(s_c, s_n, vt_c, v_n):
    m = jnp.maximum(jnp.max(s_c, axis=1, keepdims=True), jnp.max(s_n, axis=1, keepdims=True))
    p_c = jnp.exp(s_c - m)
    p_n = jnp.exp(s_n - m)
    den = jnp.sum(p_c, axis=1, keepdims=True) + jnp.sum(p_n, axis=1, keepdims=True)
    return (_dot_nt(p_c.astype(BF16), vt_c) + _dot(p_n.astype(BF16), v_n)) / den


def _sample_attn_kernel(kct_ref, vct_ref, kbct_ref, ckvc_ref, krct_ref, wukvt_ref,
                        qf2_ref, kf2_ref, vf2_ref, qm2_ref, km2_ref, vm2_ref,
                        fox_ref, mla_ref, *, past, chunk):
    n = qf2_ref.shape[1]
    row = lax.broadcasted_iota(jnp.int32, (n, n), 0)
    col = lax.broadcasted_iota(jnp.int32, (n, n), 1)
    chunk_ok = ((past + col) // chunk) <= ((past + row) // chunk)
    row2 = lax.broadcasted_iota(jnp.int32, (2 * n, n), 0) % n
    frame_ok2 = lax.broadcasted_iota(jnp.int32, (2 * n, n), 1) <= row2
    first = lax.broadcasted_iota(jnp.int32, (1, LANES), 1) < FOX_HEAD_DIM
    kbct = kbct_ref[0, 0]

    fox_scores = []
    for p in range(FOX_HEADS // 2):
        kkt = jnp.concatenate([kct_ref[0, 0, p * LANES:(p + 1) * LANES].astype(BF16), kbct], axis=0)
        qpair = qf2_ref[0, :, p * PAIR:(p + 1) * PAIR]
        q_st = jnp.concatenate([_select_head(qpair, p, sub) for sub in range(2)], axis=0)
        s_n = _dot_nt(q_st, kf2_ref[0, :, p * PAIR:(p + 1) * PAIR])
        fox_scores.append((_dot(q_st, kkt), jnp.where(frame_ok2, s_n, -1e30)))

    kvt_c = _dot_nt(wukvt_ref[...], ckvc_ref[0, 0].astype(BF16))
    krct = krct_ref[0, 0].astype(BF16)
    pad = jnp.zeros((PAIR - MLA_NOPE_DIM - MLA_ROPE_DIM, krct.shape[1]), BF16)
    mla_scores = []
    for hd in range(MLA_HEADS):
        kkt = jnp.concatenate([kvt_c[hd * LANES:(hd + 1) * LANES].astype(BF16), krct, pad], axis=0)
        q = qm2_ref[0, :, hd * PAIR:(hd + 1) * PAIR]
        s_n = jnp.where(chunk_ok, _dot_nt(q, km2_ref[0, :, hd * PAIR:(hd + 1) * PAIR]), -1e30)
        mla_scores.append((_dot(q, kkt), s_n))

    for p, (s_c, s_n) in enumerate(fox_scores):
        vvt = vct_ref[0, 0, p * LANES:(p + 1) * LANES].astype(BF16)
        o = _softmax_pv(s_c, s_n, vvt, vf2_ref[0, :, p * PAIR:p * PAIR + LANES])
        fox_ref[0, :, p * LANES:(p + 1) * LANES] = jnp.where(first, o[:n], o[n:]).astype(fox_ref.dtype)
    for hd, (s_c, s_n) in enumerate(mla_scores):
        vvt = kvt_c[MLA_HEADS * LANES + hd * LANES:MLA_HEADS * LANES + (hd + 1) * LANES].astype(BF16)
        o = _softmax_pv(s_c, s_n, vvt, vm2_ref[0, :, hd * PAIR:hd * PAIR + LANES])
        mla_ref[0, :, hd * LANES:(hd + 1) * LANES] = o.astype(mla_ref.dtype)


def _sample_attention(l, kct, vct, kbct, ckvc, krct, wukvt, qf2, kf2, vf2, qm2, km2, vm2, *, chunk):
    b, n = qf2.shape[:2]
    past = kct.shape[3]
    cache = lambda a: pl.BlockSpec((1, 1) + a.shape[2:], lambda i: (l, i, 0, 0))
    per_b = lambda a: pl.BlockSpec((1,) + a.shape[1:], lambda i: (i, 0, 0))
    return pl.pallas_call(
        functools.partial(_sample_attn_kernel, past=past, chunk=chunk),
        grid=(b,),
        in_specs=[cache(kct), cache(vct), cache(kbct), cache(ckvc), cache(krct),
                  pl.BlockSpec(wukvt.shape, lambda i: (0, 0)),
                  per_b(qf2), per_b(kf2), per_b(vf2), per_b(qm2), per_b(km2), per_b(vm2)],
        out_specs=(pl.BlockSpec((1, n, FOX_WIDTH), lambda i: (i, 0, 0)),
                   pl.BlockSpec((1, n, MLA_WIDTH), lambda i: (i, 0, 0))),
        out_shape=(jax.ShapeDtypeStruct((b, n, FOX_WIDTH), BF16), jax.ShapeDtypeStruct((b, n, MLA_WIDTH), BF16)),
        compiler_params=_cparams(("parallel",)),
        name="sample_attn",
    )(kct, vct, kbct, ckvc, krct, wukvt, qf2, kf2, vf2, qm2, km2, vm2)


def _memkv_kernel(mem_ref, g_ref, wk_ref, wv_ref, mk_ref, mv_ref, mkb_ref, mvb_ref):
    m = _rms(mem_ref[0], g_ref[0]).astype(BF16)
    mk = _dot(m, wk_ref[0])
    mv = _dot(m, wv_ref[0])
    mk_ref[0, 0] = mk
    mv_ref[0, 0] = mv
    mkb_ref[0, 0] = mk.astype(BF16)
    mvb_ref[0, 0] = mv.astype(BF16)


def _memory_kv(mem, g, wk, wv):
    b, n, d = mem.shape
    depth = g.shape[0]
    o_spec = pl.BlockSpec((1, 1, n, d), lambda l, i: (l, i, 0, 0))
    w_spec = pl.BlockSpec((1, d, d), lambda l, i: (l, 0, 0))
    return pl.pallas_call(
        _memkv_kernel,
        grid=(depth, b),
        in_specs=[pl.BlockSpec((1, n, d), lambda l, i: (i, 0, 0)),
                  pl.BlockSpec((1, 1, d), lambda l, i: (l, 0, 0)), w_spec, w_spec],
        out_specs=(o_spec,) * 4,
        out_shape=(jax.ShapeDtypeStruct((depth, b, n, d), F32),) * 2 + (jax.ShapeDtypeStruct((depth, b, n, d), BF16),) * 2,
        compiler_params=_cparams(("arbitrary", "arbitrary")),
        name="memory_kv",
    )(mem, g, wk, wv)


def _post_kernel(x_ref, fox_ref, mla_ref, wo_ref, g_ref, wxq_ref, mk_ref, mv_ref, wxo_ref, o_ref, *dense, nb):
    x = x_ref[...] + _dot(fox_ref[...], wo_ref[:FOX_WIDTH, :]) + _dot(mla_ref[...], wo_ref[FOX_WIDTH:, :])
    hc = _rms(x, g_ref[...]).astype(BF16)
    q = (_dot(hc, wxq_ref[...]) * (X_HEAD_DIM ** -0.5)).astype(BF16)
    tm = x.shape[0] // nb
    rows = []
    for bi in range(nb):
        if dense:
            for src_ref, dst_ref in zip((mk_ref, mv_ref), dense):
                for hd in range(X_HEADS):
                    dst_ref[:, hd * X_HEAD_DIM:(hd + 1) * X_HEAD_DIM] = src_ref[0, bi, :, hd, :]
            mk_b, mv_b = dense
        else:
            mk_b, mv_b = mk_ref.at[0, bi], mv_ref.at[0, bi]
        outs = []
        for hd in range(X_HEADS):
            sl = slice(hd * X_HEAD_DIM, (hd + 1) * X_HEAD_DIM)
            s = _dot_nt(q[bi * tm:(bi + 1) * tm, sl], mk_b[:, sl].astype(BF16))
            p = jnp.exp(s - jnp.max(s, axis=1, keepdims=True))
            den = jnp.sum(p, axis=1, keepdims=True)
            outs.append((_dot(p.astype(BF16), mv_b[:, sl].astype(BF16)) / den).astype(BF16))
        rows.append(jnp.concatenate(outs, axis=1))
    o = rows[0] if nb == 1 else jnp.concatenate(rows, axis=0)
    o_ref[...] = x + _dot(o, wxo_ref[...])


def _post(l, x, fox, mla, wo, g, wxq, mk, mv, wxo, *, b, tm, nb):
    n, d = x.shape
    s = n // b
    tiles = s // tm
    rows = nb * tm
    tok = lambda w: pl.BlockSpec((rows, w), lambda i, j: (i * tiles + j, 0))
    full = lambda a: pl.BlockSpec(a.shape, lambda i, j: (0,) * a.ndim)
    mem = pl.BlockSpec((1, nb) + mk.shape[2:], lambda i, j: (l, i) + (0,) * (mk.ndim - 2))
    dense = [pltpu.VMEM((mk.shape[2], d), mk.dtype)] * 2 if mk.ndim == 5 else []
    return pl.pallas_call(
        functools.partial(_post_kernel, nb=nb),
        grid=(b // nb, tiles),
        in_specs=[tok(d), tok(FOX_WIDTH), tok(MLA_WIDTH), full(wo), full(g), full(wxq), mem, mem, full(wxo)],
        out_specs=tok(d),
        out_shape=jax.ShapeDtypeStruct((n, d), F32),
        scratch_shapes=dense,
        compiler_params=_cparams(("parallel", "arbitrary")),
        name="post",
    )(x, fox, mla, wo, g, wxq, mk, mv, wxo)


def _mlp_kernel(x_ref, g_ref, wup_ref, wdn_ref, gf_ref, o_ref, *, final, ff_blk):
    x = x_ref[...]
    hm = _rms(x, g_ref[...]).astype(BF16)
    acc = x
    for c in range(D_FF // ff_blk):
        u = jnp.maximum(_dot(hm, wup_ref[:, c * ff_blk:(c + 1) * ff_blk]), 0.0)
        acc = acc + _dot(jnp.square(u).astype(BF16), wdn_ref[c * ff_blk:(c + 1) * ff_blk, :])
    o_ref[...] = _rms(acc, gf_ref[...]) if final else acc


def _mlp(x, g, wup, wdn, gf, *, final, tm):
    n, d = x.shape
    full = lambda a: pl.BlockSpec(a.shape, lambda i: (0,) * a.ndim, pipeline_mode=pl.Buffered(1))
    return pl.pallas_call(
        functools.partial(_mlp_kernel, final=final, ff_blk=1024),
        grid=(n // tm,),
        in_specs=[pl.BlockSpec((tm, d), lambda i: (i, 0)), full(g), full(wup), full(wdn), full(gf)],
        out_specs=pl.BlockSpec((tm, d), lambda i: (i, 0)),
        out_shape=jax.ShapeDtypeStruct((n, d), F32),
        compiler_params=_cparams(("parallel",)),
        name="mlp",
    )(x, g, wup, wdn, gf)


def _gate_block(a):
    pad = jnp.zeros(a.shape[:-1] + (LANES - FOX_HEADS * BIAS_SLOTS - FOX_HEADS,), a.dtype)
    return jnp.concatenate([jnp.repeat(a, BIAS_SLOTS, axis=-1), a, pad], axis=-1)


def _split_win(w):
    o = 3 * FOX_WIDTH
    o2 = o + FOX_HEADS + MLA_Q_RANK
    half = MLA_ROPE_DIM // 2
    kr = w[:, o2 + MLA_KV_RANK:]
    return dict(q=w[:, :FOX_WIDTH], k=w[:, FOX_WIDTH:2 * FOX_WIDTH], v=w[:, 2 * FOX_WIDTH:o], g=w[:, o:o + FOX_HEADS],
                cq=w[:, o + FOX_HEADS:o2], ckv=w[:, o2:o2 + MLA_KV_RANK], kr=kr,
                kr_sw=jnp.concatenate([kr[:, half:], kr[:, :half]], axis=1))


def _layout_win_sample(w):
    p = _split_win(w)
    return jnp.concatenate([p["q"], p["k"], p["v"], p["cq"], p["ckv"], p["kr"], p["kr_sw"], _gate_block(p["g"])], axis=1).astype(BF16)


def _layout_win_prompt(w):
    p = _split_win(w)
    wr = jnp.concatenate([p["k"], p["ckv"], p["kr"], p["kr_sw"], _gate_block(p["g"])], axis=1).astype(BF16)
    wt = jnp.concatenate([p["q"], p["v"], p["cq"], p["ckv"], p["kr"]], axis=1).T.astype(BF16)
    return wr, wt


def _layout_wuq_sample(w):
    w = w.reshape(MLA_Q_RANK, MLA_HEADS, MLA_NOPE_DIM + MLA_ROPE_DIM)
    nope = w[:, :, :MLA_NOPE_DIM].reshape(MLA_Q_RANK, MLA_HEADS * MLA_NOPE_DIM)
    r = w[:, :, MLA_NOPE_DIM:]
    half = MLA_ROPE_DIM // 2
    r2 = jnp.concatenate([r, r[:, :, half:], r[:, :, :half]], axis=2).reshape(MLA_Q_RANK, MLA_HEADS * LANES)
    return jnp.concatenate([nope, r2], axis=1).astype(BF16)


def _layout_wuq_prompt(w):
    w = w.reshape(MLA_Q_RANK, MLA_HEADS, MLA_NOPE_DIM + MLA_ROPE_DIM)
    return jnp.concatenate([w[:, :, :MLA_NOPE_DIM].reshape(MLA_Q_RANK, -1),
                            w[:, :, MLA_NOPE_DIM:].reshape(MLA_Q_RANK, -1)], axis=1).T.astype(BF16)


def _layout_wukv(w):
    w = w.reshape(MLA_KV_RANK, MLA_HEADS, MLA_NOPE_DIM + MLA_V_DIM)
    return jnp.concatenate([w[:, :, :MLA_NOPE_DIM].reshape(MLA_KV_RANK, -1),
                            w[:, :, MLA_NOPE_DIM:].reshape(MLA_KV_RANK, -1)], axis=1).astype(BF16)


def _rope_tables(pos):
    half = MLA_ROPE_DIM // 2
    inv = ROPE_THETA ** (-jnp.arange(half, dtype=F32) / half)
    ang = pos.astype(F32)[:, None] * inv[None, :]
    return jnp.cos(ang), jnp.sin(ang)


def _tile(n, pref):
    return pref if n % pref == 0 else n


def kernel(x_prompt, x_sample, mem_prompt, cache_fox_k, cache_fox_v, cache_fox_logf, cache_mla_ckv, cache_mla_krope,
           cache_mem_k, cache_mem_v, norm_mix, w_in, b_forget, mla_q_norm, w_uq, mla_kv_norm, w_ukv, w_out,
           norm_cross, norm_mem, w_xq, w_mk, w_mv, w_xo, norm_mlp, w_up, w_down, norm_final):
    depth = w_in.shape[0]
    bp, sp, d = x_prompt.shape
    bs, ss, _ = x_sample.shape
    past = cache_fox_k.shape[2]
    n_mem = mem_prompt.shape[1]
    row = lambda a: a.reshape(1, -1)
    col = lambda a: a.reshape(-1, 1)

    cos_p, sin_p = _rope_tables(jnp.arange(sp))
    cos_s, sin_s = _rope_tables(past + jnp.arange(ss))
    cs_p = jnp.concatenate([cos_p, cos_p, -sin_p, sin_p], axis=1)
    cs_s = jnp.tile(jnp.concatenate([cos_s, cos_s, -sin_s, sin_s], axis=1), (bs, 1))
    cos_pt, sin_pt = cos_p.T, sin_p.T

    mk_all, mv_all, mkb_all, mvb_all = _memory_kv(mem_prompt, norm_mem[:, None, :], w_mk.astype(BF16), w_mv.astype(BF16))

    kct = jnp.transpose(cache_fox_k, (0, 1, 3, 4, 2)).reshape(depth, bs, FOX_WIDTH, past)
    vct = jnp.transpose(cache_fox_v, (0, 1, 3, 4, 2)).reshape(depth, bs, FOX_WIDTH, past)
    krct = jnp.transpose(cache_mla_krope, (0, 1, 3, 2))
    kbct, tot = _cache_scan(jnp.transpose(cache_fox_logf, (0, 1, 3, 2)).astype(F32))
    cin_all = _gate_block(tot[..., 0])[:, :, None, :]

    cmk = cache_mem_k.reshape(depth, bs, n_mem, d)
    cmv = cache_mem_v.reshape(depth, bs, n_mem, d)

    tm_p = _tile(sp, 512)
    xp, xs = x_prompt, x_sample
    rows_p, rows_s = None, []
    for l in range(depth):
        wsplit_s = _layout_win_sample(w_in[l])
        wr, wt = _layout_win_prompt(w_in[l])
        bblk = _gate_block(b_forget[l][None, :])
        wukv = _layout_wukv(w_ukv[l])
        wnope, wvt = wukv[:, :MLA_HEADS * MLA_NOPE_DIM], wukv[:, MLA_HEADS * MLA_NOPE_DIM:].T
        wo, wxq, wxo = w_out[l].astype(BF16), w_xq[l].astype(BF16), w_xo[l].astype(BF16)
        wup, wdn = w_up[l].astype(BF16), w_down[l].astype(BF16)
        gmix, gq, gkv, gx, gm = row(norm_mix[l]), row(mla_q_norm[l]), row(mla_kv_norm[l]), row(norm_cross[l]), row(norm_mlp[l])
        final = l == depth - 1

        *rows_p, qt, k2, vt2, qmt, km2, vmt = _inproj_prompt(
            xp, cs_p, cos_pt, sin_pt, gmix, wr, wt, bblk, col(mla_q_norm[l]), _layout_wuq_prompt(w_uq[l]),
            gkv, col(mla_kv_norm[l]), wnope, wvt, rows_p, tm=tm_p)
        fox = _attention(qt, k2, vt2, n_chain=4, n_sub=2, chunk=1, v_rows=FOX_V_ROWS, v_dim=FOX_HEAD_DIM)
        mla = _attention(qmt, km2, vmt, n_chain=4, n_sub=1, chunk=CHUNK, v_rows=MLA_V_ROWS, v_dim=MLA_V_DIM)
        xp = _post(l, xp.reshape(bp * sp, d), fox.reshape(bp * sp, FOX_WIDTH), mla.reshape(bp * sp, MLA_WIDTH),
                   wo, gx, wxq, mkb_all, mvb_all, wxo, b=bp, tm=_tile(sp, 1024), nb=1)
        xp = _mlp(xp, gm, wup, wdn, row(norm_final), final=final, tm=_tile(bp * sp, 512)).reshape(bp, sp, d)

        ns = bs * ss
        outs = _inproj(xs.reshape(1, ns, d), cin_all[l], cs_s, gmix, wsplit_s, bblk, gq, _layout_wuq_sample(w_uq[l]),
                       gkv, wukv, tm=_tile(ns, 8 * ss), seg=ss)
        kf, vf, lf, ckv, kr, qf2, kf2, vf2, qm2, km2, vm2 = (o.reshape(bs, ss, o.shape[-1]) for o in outs)
        fox, mla = _sample_attention(l, kct, vct, kbct, cache_mla_ckv, krct, wukv.T, qf2, kf2, vf2, qm2, km2, vm2, chunk=CHUNK)
        xs = _post(l, xs.reshape(ns, d), fox.reshape(ns, FOX_WIDTH), mla.reshape(ns, MLA_WIDTH),
                   wo, gx, wxq, cache_mem_k, cache_mem_v, wxo, b=bs, tm=ss, nb=4 if bs % 4 == 0 else 1)
        xs = _mlp(xs, gm, wup, wdn, row(norm_final), final=final, tm=_tile(ns, 512)).reshape(bs, ss, d)
        rows_s.append((kf, vf, lf, ckv, kr))

    st = lambda rows, i: jnp.stack([r[i] for r in rows])
    heads5 = lambda a: jnp.transpose(a.reshape(depth, bp, FOX_HEADS, FOX_HEAD_DIM, sp), (0, 1, 4, 2, 3))
    lo = FOX_HEADS * BIAS_SLOTS
    mem_shape = (depth, bp, n_mem, X_HEADS, X_HEAD_DIM)
    kt_all, vt_all, lft_all, ckv_all, krt_all = rows_p
    return (xp, xs,
            heads5(kt_all), heads5(vt_all), jnp.transpose(lft_all, (0, 1, 3, 2)), ckv_all,
            jnp.transpose(krt_all, (0, 1, 3, 2)),
            mk_all.reshape(mem_shape), mv_all.reshape(mem_shape),
            st(rows_s, 0).reshape(depth, bs, ss, FOX_HEADS, FOX_HEAD_DIM), st(rows_s, 1).reshape(depth, bs, ss, FOX_HEADS, FOX_HEAD_DIM),
            st(rows_s, 2)[..., lo:lo + FOX_HEADS], st(rows_s, 3), st(rows_s, 4))
```

```python
import functools

import jax
import jax.numpy as jnp
from jax import lax
from jax.experimental import pallas as pl
from jax.experimental.pallas import tpu as pltpu

F32 = jnp.float32
BF16 = jnp.bfloat16

D_MODEL = 1024
CHUNK = 64
FOX_HEADS = 8
FOX_HEAD_DIM = 64
FOX_WIDTH = FOX_HEADS * FOX_HEAD_DIM
MLA_HEADS = 4
MLA_NOPE_DIM = 128
MLA_ROPE_DIM = 64
MLA_V_DIM = 128
MLA_WIDTH = MLA_HEADS * MLA_V_DIM
MLA_Q_RANK = 256
MLA_KV_RANK = 128
ROPE_THETA = 10000.0
X_HEADS = 4
X_HEAD_DIM = 256
D_FF = 4 * D_MODEL
EPS = 1e-6

LOG2E = 1.4426950408889634
LANES = 128
SUBLANES = 8
BF16_ROWS = 16
PAIR = 2 * LANES
BIAS_SLOTS = 8
VMEM_LIMIT = 56 * 1024 * 1024
FOX_V_ROWS = FOX_HEAD_DIM + BF16_ROWS
MLA_V_ROWS = MLA_V_DIM + BF16_ROWS

_OQ, _OK, _OV = 0, FOX_WIDTH, 2 * FOX_WIDTH
_OCQ = 3 * FOX_WIDTH
_OCKV = _OCQ + MLA_Q_RANK
_OKR = _OCKV + MLA_KV_RANK
_OG = _OKR + LANES
_RK, _RCKV, _RKR, _RG = 0, FOX_WIDTH, FOX_WIDTH + MLA_KV_RANK, FOX_WIDTH + MLA_KV_RANK + LANES
_TQ, _TV = 0, FOX_WIDTH
_TCQ = 2 * FOX_WIDTH
_TCKV = _TCQ + MLA_Q_RANK
_TKR = _TCKV + MLA_KV_RANK


def _cparams(sem):
    return pltpu.CompilerParams(dimension_semantics=sem, vmem_limit_bytes=VMEM_LIMIT)


def _rms(x, g):
    return x * lax.rsqrt(jnp.mean(jnp.square(x), axis=-1, keepdims=True) + EPS) * g


def _rms_rows(x, g):
    return x * lax.rsqrt(jnp.mean(jnp.square(x), axis=0, keepdims=True) + EPS) * g


def _dot(a, b):
    return jnp.dot(a, b, preferred_element_type=F32)


def _dot_nt(a, b):
    return lax.dot_general(a, b, (((1,), (1,)), ((), ())), preferred_element_type=F32)


def _split3(x):
    x1 = x.astype(BF16)
    r1 = x - x1.astype(F32)
    x2 = r1.astype(BF16)
    r2 = r1 - x2.astype(F32)
    return x1, x2, r2.astype(BF16)


def _cumsum_rows(x, carry):
    n = x.shape[0]
    blk = min(n, 256)
    r = lax.broadcasted_iota(jnp.int32, (blk, blk), 0)
    c = lax.broadcasted_iota(jnp.int32, (blk, blk), 1)
    tri = (c <= r).astype(BF16)
    outs = []
    for i in range(n // blk):
        x1, x2, x3 = _split3(x[i * blk:(i + 1) * blk])
        cb = _dot(tri, x1) + _dot(tri, x2) + _dot(tri, x3) + carry
        carry = cb[blk - 1:blk]
        outs.append(cb)
    return outs[0] if len(outs) == 1 else jnp.concatenate(outs, axis=0)


def _cumsum_lanes(x):
    n = x.shape[1]
    blk = min(n, 256)
    r = lax.broadcasted_iota(jnp.int32, (blk, blk), 0)
    c = lax.broadcasted_iota(jnp.int32, (blk, blk), 1)
    tri = (r <= c).astype(BF16)
    carry = jnp.zeros((x.shape[0], 1), F32)
    outs = []
    for i in range(n // blk):
        x1, x2, x3 = _split3(x[:, i * blk:(i + 1) * blk])
        cb = _dot(x1, tri) + _dot(x2, tri) + _dot(x3, tri) + carry
        carry = cb[:, blk - 1:blk]
        outs.append(cb)
    return outs[0] if len(outs) == 1 else jnp.concatenate(outs, axis=1)


def _bias_blocks(cum, slot_axis):
    c1, c2, c3 = (c.astype(F32) for c in _split3(cum))
    slot = lax.broadcasted_iota(jnp.int32, cum.shape, slot_axis)
    j = slot % BIAS_SLOTS
    valid = slot < FOX_HEADS * BIAS_SLOTS
    one = jnp.where(valid & (j < 6), 1.0, 0.0)
    qb = jnp.where(j == 0, c1, jnp.where(j == 1, c2, jnp.where(j == 2, c3, one)))
    kb = jnp.where(j < 3, one, jnp.where(j == 3, -c1, jnp.where(j == 4, -c2, jnp.where(j == 5, -c3, 0.0))))
    return jnp.where(valid, qb, 0.0), jnp.where(valid, kb, 0.0)


def _log_sigmoid(x):
    return jnp.minimum(x, 0.0) - jnp.log1p(jnp.exp(-jnp.abs(x)))


def _rope_pair(t, cs):
    u = t * cs
    return u + pltpu.roll(u, MLA_ROPE_DIM, 1)


def _rope_rows(x, cos, sin):
    half = MLA_ROPE_DIM // 2
    x1, x2 = x[:half], x[half:]
    return jnp.concatenate([x1 * cos - x2 * sin, x2 * cos + x1 * sin], axis=0)


def _inproj_kernel(x_ref, cin_ref, cs_ref, gmix_ref, win_ref, bblk_ref, gq_ref, wuq_ref, gkv_ref, wukv_ref,
                   kf_ref, vf_ref, lf_ref, ckv_ref, kr_ref,
                   qf2_ref, kf2_ref, vf2_ref, qm2_ref, km2_ref, vm2_ref, *, seg):
    x = x_ref[0]
    tm = x.shape[0]
    h = _rms(x, gmix_ref[...])
    z = _dot(h.astype(BF16), win_ref[...])
    cs = cs_ref[...]
    lane = lax.broadcasted_iota(jnp.int32, (1, LANES), 1)
    low = lane < MLA_ROPE_DIM
    ones_blk = jnp.ones((tm, LANES), BF16)

    kf_ref[0] = z[:, _OK:_OK + FOX_WIDTH]
    vf_ref[0] = z[:, _OV:_OV + FOX_WIDTH]

    lf = _log_sigmoid(z[:, _OG:_OG + LANES] + bblk_ref[...])
    lf_ref[0] = lf
    cums = [_cumsum_rows(lf[i * seg:(i + 1) * seg], cin_ref[i]) for i in range(tm // seg)]
    cum = cums[0] if len(cums) == 1 else jnp.concatenate(cums, axis=0)
    qb, kb = _bias_blocks(cum, 1)

    fox_scale = FOX_HEAD_DIM ** -0.5
    for p in range(FOX_HEADS // 2):
        lo, hi = p * PAIR, p * PAIR + LANES
        qf2_ref[0, :, lo:hi] = (z[:, _OQ + p * LANES:_OQ + (p + 1) * LANES] * fox_scale).astype(BF16)
        qf2_ref[0, :, hi:hi + LANES] = qb.astype(BF16)
        kf2_ref[0, :, lo:hi] = z[:, _OK + p * LANES:_OK + (p + 1) * LANES].astype(BF16)
        kf2_ref[0, :, hi:hi + LANES] = kb.astype(BF16)
        vf2_ref[0, :, lo:hi] = z[:, _OV + p * LANES:_OV + (p + 1) * LANES].astype(BF16)
        vf2_ref[0, :, hi:hi + LANES] = ones_blk

    kr = _rope_pair(z[:, _OKR:_OKR + LANES], cs)
    kr_ref[0] = kr[:, :MLA_ROPE_DIM]
    krb = jnp.where(low, kr, 0.0).astype(BF16)

    ckv = _rms(z[:, _OCKV:_OCKV + MLA_KV_RANK], gkv_ref[...])
    ckv_ref[0] = ckv
    kv = _dot(ckv.astype(BF16), wukv_ref[...])
    cqn = _rms(z[:, _OCQ:_OCQ + MLA_Q_RANK], gq_ref[...])
    qm = _dot(cqn.astype(BF16), wuq_ref[...])
    mla_scale = (MLA_NOPE_DIM + MLA_ROPE_DIM) ** -0.5
    for hd in range(MLA_HEADS):
        lo, hi = hd * PAIR, hd * PAIR + LANES
        qr = _rope_pair(qm[:, MLA_HEADS * LANES + hd * LANES:MLA_HEADS * LANES + (hd + 1) * LANES], cs)
        qm2_ref[0, :, lo:hi] = (qm[:, hd * LANES:(hd + 1) * LANES] * mla_scale).astype(BF16)
        qm2_ref[0, :, hi:hi + LANES] = (jnp.where(low, qr, 0.0) * mla_scale).astype(BF16)
        km2_ref[0, :, lo:hi] = kv[:, hd * LANES:(hd + 1) * LANES].astype(BF16)
        km2_ref[0, :, hi:hi + LANES] = krb
        vm2_ref[0, :, lo:hi] = kv[:, MLA_HEADS * LANES + hd * LANES:MLA_HEADS * LANES + (hd + 1) * LANES].astype(BF16)
        vm2_ref[0, :, hi:hi + LANES] = ones_blk


def _inproj(x, cin, cs, gmix, win, bblk, gq, wuq, gkv, wukv, *, tm, seg):
    b, s, _ = x.shape
    grid = (b, s // tm)
    tok = lambda w: pl.BlockSpec((1, tm, w), lambda i, j: (i, j, 0))
    full = lambda a: pl.BlockSpec(a.shape, lambda i, j: (0,) * a.ndim)
    out_shape = (
        jax.ShapeDtypeStruct((b, s, FOX_WIDTH), F32), jax.ShapeDtypeStruct((b, s, FOX_WIDTH), F32),
        jax.ShapeDtypeStruct((b, s, LANES), F32), jax.ShapeDtypeStruct((b, s, MLA_KV_RANK), F32),
        jax.ShapeDtypeStruct((b, s, MLA_ROPE_DIM), F32),
    ) + tuple(jax.ShapeDtypeStruct((b, s, 4 * PAIR), BF16) for _ in range(6))
    out_specs = (tok(FOX_WIDTH), tok(FOX_WIDTH), tok(LANES), tok(MLA_KV_RANK), tok(MLA_ROPE_DIM)) + (tok(4 * PAIR),) * 6
    return pl.pallas_call(
        functools.partial(_inproj_kernel, seg=seg),
        grid=grid,
        in_specs=[tok(D_MODEL), pl.BlockSpec((tm // seg, 1, LANES), lambda i, j: (j, 0, 0)),
                  pl.BlockSpec((tm, LANES), lambda i, j: (j, 0)),
                  full(gmix), full(win), full(bblk), full(gq), full(wuq), full(gkv), full(wukv)],
        out_specs=out_specs,
        out_shape=out_shape,
        compiler_params=_cparams(("parallel", "parallel")),
        name="inproj_sample",
    )(x, cin, cs, gmix, win, bblk, gq, wuq, gkv, wukv)


def _inproj_prompt_kernel(*refs, n_prev):
    (x_ref, cs_ref, cos_ref, sin_ref, gmix_ref, wr_ref, wt_ref, bblk_ref, gq_ref, wuqt_ref, gkv_ref, gkvc_ref,
     wnope_ref, wvt_ref) = refs[:14]
    prev_refs = refs[14:19] if n_prev else ()
    (kt_ref, vt_ref, lft_ref, ckv_ref, krt_ref,
     qt_ref, k2_ref, vt2_ref, qmt_ref, km2_ref, vmt_ref, carry_ref) = refs[14 + len(prev_refs):]

    @pl.when(pl.program_id(1) == 0)
    def _():
        carry_ref[...] = jnp.zeros_like(carry_ref)

    x = x_ref[0]
    tm = x.shape[0]
    hb = _rms(x, gmix_ref[...]).astype(BF16)
    zr = _dot(hb, wr_ref[...])
    zt = _dot_nt(wt_ref[...], hb)
    cs, cos, sin = cs_ref[...], cos_ref[...], sin_ref[...]
    ones_rows = jnp.ones((BF16_ROWS, tm), BF16)

    for prev_ref, out_ref in zip(prev_refs, (kt_ref, vt_ref, lft_ref, ckv_ref, krt_ref)):
        out_ref[:n_prev, 0] = prev_ref[:, 0]
    kt_ref[n_prev, 0] = zr[:, _RK:_RK + FOX_WIDTH].T
    vt_ref[n_prev, 0] = zt[_TV:_TV + FOX_WIDTH]

    lf = _log_sigmoid(zr[:, _RG:_RG + LANES] + bblk_ref[...])
    cum = _cumsum_rows(lf, carry_ref[...])
    carry_ref[...] = cum[tm - 1:tm]
    qb, kb = _bias_blocks(cum * LOG2E, 1)
    kb = kb.astype(BF16)
    qbt = qb.T
    lo = FOX_HEADS * BIAS_SLOTS
    lft_ref[n_prev, 0] = lf.T[lo:lo + FOX_HEADS]

    fox_scale = FOX_HEAD_DIM ** -0.5 * LOG2E
    rowi = lax.broadcasted_iota(jnp.int32, (LANES, tm), 0)
    for p in range(FOX_HEADS // 2):
        zq = zt[_TQ + p * LANES:_TQ + (p + 1) * LANES] * fox_scale
        for sub in range(2):
            hd = 2 * p + sub
            qt_ref[0, 0, hd * PAIR:hd * PAIR + LANES] = jnp.where(rowi // FOX_HEAD_DIM == sub, zq, 0.0).astype(BF16)
            qt_ref[0, 0, hd * PAIR + LANES:(hd + 1) * PAIR] = jnp.where(rowi // BIAS_SLOTS == hd, qbt, 0.0).astype(BF16)
        k2_ref[0, :, p * PAIR:p * PAIR + LANES] = zr[:, _RK + p * LANES:_RK + (p + 1) * LANES].astype(BF16)
        k2_ref[0, :, p * PAIR + LANES:(p + 1) * PAIR] = kb
    for hd in range(FOX_HEADS):
        r0 = hd * FOX_V_ROWS
        vt2_ref[0, 0, r0:r0 + FOX_HEAD_DIM] = zt[_TV + hd * FOX_HEAD_DIM:_TV + (hd + 1) * FOX_HEAD_DIM].astype(BF16)
        vt2_ref[0, 0, r0 + FOX_HEAD_DIM:r0 + FOX_V_ROWS] = ones_rows

    krt_ref[n_prev, 0] = _rope_rows(zt[_TKR:_TKR + MLA_ROPE_DIM], cos, sin)
    kr = _rope_pair(zr[:, _RKR:_RKR + LANES], cs)
    krb = jnp.where(lax.broadcasted_iota(jnp.int32, (1, LANES), 1) < MLA_ROPE_DIM, kr, 0.0).astype(BF16)
    ckv = _rms(zr[:, _RCKV:_RCKV + MLA_KV_RANK], gkv_ref[...])
    ckv_ref[n_prev, 0] = ckv
    knope = _dot(ckv.astype(BF16), wnope_ref[...])
    ckvt = _rms_rows(zt[_TCKV:_TCKV + MLA_KV_RANK], gkvc_ref[...])
    vmt = _dot(wvt_ref[...], ckvt.astype(BF16))
    cqnt = _rms_rows(zt[_TCQ:_TCQ + MLA_Q_RANK], gq_ref[...])
    qmt = _dot(wuqt_ref[...], cqnt.astype(BF16))
    mla_scale = (MLA_NOPE_DIM + MLA_ROPE_DIM) ** -0.5 * LOG2E
    zeros_rows = jnp.zeros((PAIR - MLA_NOPE_DIM - MLA_ROPE_DIM, tm), BF16)
    for hd in range(MLA_HEADS):
        q0 = hd * PAIR
        qmt_ref[0, 0, q0:q0 + MLA_NOPE_DIM] = (qmt[hd * MLA_NOPE_DIM:(hd + 1) * MLA_NOPE_DIM] * mla_scale).astype(BF16)
        r0 = MLA_HEADS * MLA_NOPE_DIM + hd * MLA_ROPE_DIM
        qmt_ref[0, 0, q0 + MLA_NOPE_DIM:q0 + MLA_NOPE_DIM + MLA_ROPE_DIM] = (
            _rope_rows(qmt[r0:r0 + MLA_ROPE_DIM], cos, sin) * mla_scale).astype(BF16)
        qmt_ref[0, 0, q0 + MLA_NOPE_DIM + MLA_ROPE_DIM:q0 + PAIR] = zeros_rows
        km2_ref[0, :, q0:q0 + LANES] = knope[:, hd * LANES:(hd + 1) * LANES].astype(BF16)
        km2_ref[0, :, q0 + LANES:q0 + PAIR] = krb
        v0 = hd * MLA_V_ROWS
        vmt_ref[0, 0, v0:v0 + MLA_V_DIM] = vmt[hd * MLA_V_DIM:(hd + 1) * MLA_V_DIM].astype(BF16)
        vmt_ref[0, 0, v0 + MLA_V_DIM:v0 + MLA_V_ROWS] = ones_rows


def _inproj_prompt(x, cs, cos_t, sin_t, gmix, wr, wt, bblk, gq_col, wuqt, gkv, gkv_col, wnope, wvt, prev, *, tm):
    b, s, _ = x.shape
    nt = s // tm
    n_prev = 0 if prev is None else prev[0].shape[0]
    full = lambda a: pl.BlockSpec(a.shape, lambda i, j: (0,) * a.ndim, pipeline_mode=pl.Buffered(1))
    tile4 = lambda rows: pl.BlockSpec((1, 1, rows, tm), lambda i, j: (i, j, 0, 0))
    tok = lambda w: pl.BlockSpec((1, tm, w), lambda i, j: (i, j, 0))
    feat = lambda n, rows: pl.BlockSpec((n, 1, rows, tm), lambda i, j: (0, i, 0, j))
    toks = lambda n, w: pl.BlockSpec((n, 1, tm, w), lambda i, j: (0, i, j, 0))
    row_specs = lambda n: [feat(n, FOX_WIDTH), feat(n, FOX_WIDTH), feat(n, FOX_HEADS), toks(n, MLA_KV_RANK),
                           feat(n, MLA_ROPE_DIM)]
    ins = [x, cs, cos_t, sin_t, gmix, wr, wt, bblk, gq_col, wuqt, gkv, gkv_col, wnope, wvt]
    in_specs = [tok(D_MODEL), pl.BlockSpec((tm, LANES), lambda i, j: (j, 0)),
                pl.BlockSpec((cos_t.shape[0], tm), lambda i, j: (0, j)),
                pl.BlockSpec((sin_t.shape[0], tm), lambda i, j: (0, j))] + [full(a) for a in ins[4:]]
    if n_prev:
        ins += list(prev)
        in_specs += row_specs(n_prev)
    op4 = lambda rows: jax.ShapeDtypeStruct((b, nt, rows, tm), BF16)
    nl = n_prev + 1
    out_shape = (jax.ShapeDtypeStruct((nl, b, FOX_WIDTH, s), F32),) * 2 + (
        jax.ShapeDtypeStruct((nl, b, FOX_HEADS, s), F32), jax.ShapeDtypeStruct((nl, b, s, MLA_KV_RANK), F32),
        jax.ShapeDtypeStruct((nl, b, MLA_ROPE_DIM, s), F32),
        op4(FOX_HEADS * PAIR), jax.ShapeDtypeStruct((b, s, 4 * PAIR), BF16), op4(FOX_HEADS * FOX_V_ROWS),
        op4(MLA_HEADS * PAIR), jax.ShapeDtypeStruct((b, s, 4 * PAIR), BF16), op4(MLA_HEADS * MLA_V_ROWS))
    out_specs = tuple(row_specs(nl)) + (
        tile4(FOX_HEADS * PAIR), tok(4 * PAIR), tile4(FOX_HEADS * FOX_V_ROWS),
        tile4(MLA_HEADS * PAIR), tok(4 * PAIR), tile4(MLA_HEADS * MLA_V_ROWS))
    return pl.pallas_call(
        functools.partial(_inproj_prompt_kernel, n_prev=n_prev),
        grid=(b, nt),
        in_specs=in_specs,
        out_specs=out_specs,
        out_shape=out_shape,
        scratch_shapes=[pltpu.VMEM((1, LANES), F32)],
        compiler_params=_cparams(("parallel", "arbitrary")),
        name="inproj_prompt",
    )(*ins)


def _attn_kernel(qt_ref, k_ref, vt_ref, o_ref, sa_ref, sb_ref, sd_ref, m_ref, acc_ref, *,
                 t, nt, n_chain, n_sub, chunk, v_rows, v_dim):
    i = pl.program_id(2)
    key = lax.broadcasted_iota(jnp.int32, (t, t), 0)
    qry = lax.broadcasted_iota(jnp.int32, (t, t), 1)
    allowed = (key // chunk) <= (qry // chunk)

    def scores(qblk, kblk, dst_ref, diagonal=False):
        off = pl.multiple_of(kblk * t, t)
        for c in range(n_chain):
            g = c // n_sub
            st = _dot(k_ref[0, pl.ds(off, t), g * PAIR:(g + 1) * PAIR], qt_ref[0, qblk, c * PAIR:(c + 1) * PAIR])
            if diagonal:
                st = jnp.where(allowed, st, -1e30)
            dst_ref[c, :t] = st
            dst_ref[c, t:] = jnp.broadcast_to(jnp.max(st, axis=0, keepdims=True), (SUBLANES, t))

    def consume(kblk, src_ref):
        for c in range(n_chain):
            st = src_ref[c, :t]
            m = m_ref[c]
            m_new = jnp.maximum(m, src_ref[c, t:t + 1])
            pt = jnp.exp2(st - m_new).astype(BF16)
            acc_ref[c] = jnp.exp2(m - m_new) * acc_ref[c] + _dot(vt_ref[0, kblk, c * v_rows:(c + 1) * v_rows], pt)
            m_ref[c] = m_new

    def next_diagonal():
        nxt = jnp.minimum(i + 1, nt - 1)
        scores(nxt, nxt, sd_ref, diagonal=True)

    @pl.when(i == 0)
    def _():
        scores(0, 0, sd_ref, diagonal=True)

    m_ref[...] = jnp.full(m_ref.shape, -1e30, F32)
    acc_ref[...] = jnp.zeros(acc_ref.shape, F32)

    @pl.when(i == 0)
    def _():
        consume(0, sd_ref)
        next_diagonal()

    @pl.when(i > 0)
    def _():
        scores(i, 0, sa_ref)
        consume(i, sd_ref)
        n_pairs = (i - 1) // 2

        def pair(p, carry):
            scores(i, 2 * p + 1, sb_ref)
            consume(2 * p, sa_ref)
            scores(i, 2 * p + 2, sa_ref)
            consume(2 * p + 1, sb_ref)
            return carry

        lax.fori_loop(0, n_pairs, pair, 0)
        one_more = (i - 1) % 2 == 1

        @pl.when(one_more)
        def _():
            scores(i, i - 1, sb_ref)
            consume(2 * n_pairs, sa_ref)
            next_diagonal()
            consume(i - 1, sb_ref)

        @pl.when(jnp.logical_not(one_more))
        def _():
            next_diagonal()
            consume(2 * n_pairs, sa_ref)

    for g in range(n_chain // n_sub):
        outs = []
        for sub in range(n_sub):
            acc = acc_ref[g * n_sub + sub]
            outs.append(acc[:v_dim] / acc[v_dim:v_dim + 1])
        ot = outs[0] if n_sub == 1 else jnp.concatenate(outs, axis=0)
        o_ref[0, :, g * LANES:(g + 1) * LANES] = ot.T.astype(o_ref.dtype)


def _attention(qt, k2, vt, *, n_chain, n_sub, chunk, v_rows, v_dim):
    b, nt, rows, t = qt.shape
    s = nt * t
    steps = rows // (n_chain * PAIR)
    gps = n_chain // n_sub
    return pl.pallas_call(
        functools.partial(_attn_kernel, t=t, nt=nt, n_chain=n_chain, n_sub=n_sub, chunk=chunk, v_rows=v_rows, v_dim=v_dim),
        grid=(b, steps, nt),
        in_specs=[pl.BlockSpec((1, nt, n_chain * PAIR, t), lambda bi, g, i: (bi, 0, g, 0)),
                  pl.BlockSpec((1, s, gps * PAIR), lambda bi, g, i: (bi, 0, g)),
                  pl.BlockSpec((1, nt, n_chain * v_rows, t), lambda bi, g, i: (bi, 0, g, 0),
                               pipeline_mode=pl.Buffered(1))],
        out_specs=pl.BlockSpec((1, t, gps * LANES), lambda bi, g, i: (bi, i, g)),
        out_shape=jax.ShapeDtypeStruct((b, s, 4 * LANES), BF16),
        scratch_shapes=[pltpu.VMEM((n_chain, t + SUBLANES, t), F32)] * 3
                       + [pltpu.VMEM((n_chain, 1, t), F32), pltpu.VMEM((n_chain, v_rows, t), F32)],
        compiler_params=_cparams(("parallel", "arbitrary", "arbitrary")),
        name="attn_fox" if n_sub > 1 else "attn_mla",
    )(qt, k2, vt)


def _cache_scan_kernel(lft_ref, kbct_ref, tot_ref):
    lf = lft_ref[0, 0]
    heads, n = lf.shape
    cum = _cumsum_lanes(jnp.concatenate([lf, jnp.zeros_like(lf)], axis=0))[:heads]
    tot_ref[0, 0] = jnp.broadcast_to(cum[:, n - 1:n], (heads, LANES))
    rep = jnp.concatenate([jnp.broadcast_to(cum[hd:hd + 1], (BIAS_SLOTS, n)) for hd in range(heads)]
                          + [jnp.zeros((LANES - heads * BIAS_SLOTS, n), F32)], axis=0)
    kbct_ref[0, 0] = _bias_blocks(rep, 0)[1].astype(BF16)


def _cache_scan(lft):
    depth, b, heads, n = lft.shape
    return pl.pallas_call(
        _cache_scan_kernel,
        grid=(depth, b),
        in_specs=[pl.BlockSpec((1, 1, heads, n), lambda l, i: (l, i, 0, 0))],
        out_specs=(pl.BlockSpec((1, 1, LANES, n), lambda l, i: (l, i, 0, 0)),
                   pl.BlockSpec((1, 1, heads, LANES), lambda l, i: (l, i, 0, 0))),
        out_shape=(jax.ShapeDtypeStruct((depth, b, LANES, n), BF16), jax.ShapeDtypeStruct((depth, b, heads, LANES), F32)),
        compiler_params=_cparams(("parallel", "parallel")),
        name="cache_scan",
    )(lft)


def _select_head(q2, g, sub):
    lane = lax.broadcasted_iota(jnp.int32, (1, PAIR), 1)
    payload = (lane < LANES) & (lane // FOX_HEAD_DIM == sub)
    side = (lane >= LANES) & ((lane - LANES) // BIAS_SLOTS == 2 * g + sub)
    return jnp.where(payload | side, q2.astype(F32), 0.0).astype(BF16)


def _softmax_pv(s_c, s_n, vt_c, v_n):
    m = jnp.maximum(jnp.max(s_c, axis=1, keepdims=True), jnp.max(s_n, axis=1, keepdims=True))
    p_c = jnp.exp(s_c - m)
    p_n = jnp.exp(s_n - m)
    den = jnp.sum(p_c, axis=1, keepdims=True) + jnp.sum(p_n, axis=1, keepdims=True)
    return (_dot_nt(p_c.astype(BF16), vt_c) + _dot(p_n.astype(BF16), v_n)) / den


def _sample_attn_kernel(kct_ref, vct_ref, kbct_ref, ckvc_ref, krct_ref, wukvt_ref,
                        qf2_ref, kf2_ref, vf2_ref, qm2_ref, km2_ref, vm2_ref,
                        fox_ref, mla_ref, *, past, chunk):
    n = qf2_ref.shape[1]
    row = lax.broadcasted_iota(jnp.int32, (n, n), 0)
    col = lax.broadcasted_iota(jnp.int32, (n, n), 1)
    chunk_ok = ((past + col) // chunk) <= ((past + row) // chunk)
    row2 = lax.broadcasted_iota(jnp.int32, (2 * n, n), 0) % n
    frame_ok2 = lax.broadcasted_iota(jnp.int32, (2 * n, n), 1) <= row2
    first = lax.broadcasted_iota(jnp.int32, (1, LANES), 1) < FOX_HEAD_DIM
    kbct = kbct_ref[0, 0]

    fox_scores = []
    for p in range(FOX_HEADS // 2):
        kkt = jnp.concatenate([kct_ref[0, 0, p * LANES:(p + 1) * LANES].astype(BF16), kbct], axis=0)
        qpair = qf2_ref[0, :, p * PAIR:(p + 1) * PAIR]
        q_st = jnp.concatenate([_select_head(qpair, p, sub) for sub in range(2)], axis=0)
        s_n = _dot_nt(q_st, kf2_ref[0, :, p * PAIR:(p + 1) * PAIR])
        fox_scores.append((_dot(q_st, kkt), jnp.where(frame_ok2, s_n, -1e30)))

    kvt_c = _dot_nt(wukvt_ref[...], ckvc_ref[0, 0].astype(BF16))
    krct = krct_ref[0, 0].astype(BF16)
    pad = jnp.zeros((PAIR - MLA_NOPE_DIM - MLA_ROPE_DIM, krct.shape[1]), BF16)
    mla_scores = []
    for hd in range(MLA_HEADS):
        kkt = jnp.concatenate([kvt_c[hd * LANES:(hd + 1) * LANES].astype(BF16), krct, pad], axis=0)
        q = qm2_ref[0, :, hd * PAIR:(hd + 1) * PAIR]
        s_n = jnp.where(chunk_ok, _dot_nt(q, km2_ref[0, :, hd * PAIR:(hd + 1) * PAIR]), -1e30)
        mla_scores.append((_dot(q, kkt), s_n))

    for p, (s_c, s_n) in enumerate(fox_scores):
        vvt = vct_ref[0, 0, p * LANES:(p + 1) * LANES].astype(BF16)
        o = _softmax_pv(s_c, s_n, vvt, vf2_ref[0, :, p * PAIR:p * PAIR + LANES])
        fox_ref[0, :, p * LANES:(p + 1) * LANES] = jnp.where(first, o[:n], o[n:]).astype(fox_ref.dtype)
    for hd, (s_c, s_n) in enumerate(mla_scores):
        vvt = kvt_c[MLA_HEADS * LANES + hd * LANES:MLA_HEADS * LANES + (hd + 1) * LANES].astype(BF16)
        o = _softmax_pv(s_c, s_n, vvt, vm2_ref[0, :, hd * PAIR:hd * PAIR + LANES])
        mla_ref[0, :, hd * LANES:(hd + 1) * LANES] = o.astype(mla_ref.dtype)


def _sample_attention(l, kct, vct, kbct, ckvc, krct, wukvt, qf2, kf2, vf2, qm2, km2, vm2, *, chunk):
    b, n = qf2.shape[:2]
    past = kct.shape[3]
    cache = lambda a: pl.BlockSpec((1, 1) + a.shape[2:], lambda i: (l, i, 0, 0))
    per_b = lambda a: pl.BlockSpec((1,) + a.shape[1:], lambda i: (i, 0, 0))
    return pl.pallas_call(
        functools.partial(_sample_attn_kernel, past=past, chunk=chunk),
        grid=(b,),
        in_specs=[cache(kct), cache(vct), cache(kbct), cache(ckvc), cache(krct),
                  pl.BlockSpec(wukvt.shape, lambda i: (0, 0)),
                  per_b(qf2), per_b(kf2), per_b(vf2), per_b(qm2), per_b(km2), per_b(vm2)],
        out_specs=(pl.BlockSpec((1, n, FOX_WIDTH), lambda i: (i, 0, 0)),
                   pl.BlockSpec((1, n, MLA_WIDTH), lambda i: (i, 0, 0))),
        out_shape=(jax.ShapeDtypeStruct((b, n, FOX_WIDTH), BF16), jax.ShapeDtypeStruct((b, n, MLA_WIDTH), BF16)),
        compiler_params=_cparams(("parallel",)),
        name="sample_attn",
    )(kct, vct, kbct, ckvc, krct, wukvt, qf2, kf2, vf2, qm2, km2, vm2)


def _memkv_kernel(mem_ref, g_ref, wk_ref, wv_ref, mk_ref, mv_ref, mkb_ref, mvb_ref):
    m = _rms(mem_ref[0], g_ref[0]).astype(BF16)
    mk = _dot(m, wk_ref[0])
    mv = _dot(m, wv_ref[0])
    for hd in range(X_HEADS):
        mk_ref[0, 0, :, hd, :] = mk[:, hd * X_HEAD_DIM:(hd + 1) * X_HEAD_DIM]
        mv_ref[0, 0, :, hd, :] = mv[:, hd * X_HEAD_DIM:(hd + 1) * X_HEAD_DIM]
    mkb_ref[0, 0] = mk.astype(BF16)
    mvb_ref[0, 0] = mv.astype(BF16)


def _memory_kv(mem, g, wk, wv):
    b, n, d = mem.shape
    depth = g.shape[0]
    o_spec = pl.BlockSpec((1, 1, n, d), lambda l, i: (l, i, 0, 0))
    o5_spec = pl.BlockSpec((1, 1, n, X_HEADS, X_HEAD_DIM), lambda l, i: (l, i, 0, 0, 0))
    w_spec = pl.BlockSpec((1, d, d), lambda l, i: (l, 0, 0))
    return pl.pallas_call(
        _memkv_kernel,
        grid=(depth, b),
        in_specs=[pl.BlockSpec((1, n, d), lambda l, i: (i, 0, 0)),
                  pl.BlockSpec((1, 1, d), lambda l, i: (l, 0, 0)), w_spec, w_spec],
        out_specs=(o5_spec, o5_spec, o_spec, o_spec),
        out_shape=(jax.ShapeDtypeStruct((depth, b, n, X_HEADS, X_HEAD_DIM), F32),) * 2
                  + (jax.ShapeDtypeStruct((depth, b, n, d), BF16),) * 2,
        compiler_params=_cparams(("arbitrary", "arbitrary")),
        name="memory_kv",
    )(mem, g, wk, wv)


def _post_kernel(x_ref, fox_ref, mla_ref, wo_ref, g_ref, wxq_ref, mk_ref, mv_ref, wxo_ref, o_ref, *dense, nb):
    x = x_ref[...] + _dot(fox_ref[...], wo_ref[:FOX_WIDTH, :]) + _dot(mla_ref[...], wo_ref[FOX_WIDTH:, :])
    hc = _rms(x, g_ref[...]).astype(BF16)
    q = (_dot(hc, wxq_ref[...]) * (X_HEAD_DIM ** -0.5)).astype(BF16)
    tm = x.shape[0] // nb
    rows = []
    for bi in range(nb):
        if dense:
            for src_ref, dst_ref in zip((mk_ref, mv_ref), dense):
                for hd in range(X_HEADS):
                    dst_ref[:, hd * X_HEAD_DIM:(hd + 1) * X_HEAD_DIM] = src_ref[0, bi, :, hd, :]
            mk_b, mv_b = dense
        else:
            mk_b, mv_b = mk_ref.at[0, bi], mv_ref.at[0, bi]
        outs = []
        for hd in range(X_HEADS):
            sl = slice(hd * X_HEAD_DIM, (hd + 1) * X_HEAD_DIM)
            s = _dot_nt(q[bi * tm:(bi + 1) * tm, sl], mk_b[:, sl].astype(BF16))
            p = jnp.exp(s - jnp.max(s, axis=1, keepdims=True))
            den = jnp.sum(p, axis=1, keepdims=True)
            outs.append((_dot(p.astype(BF16), mv_b[:, sl].astype(BF16)) / den).astype(BF16))
        rows.append(jnp.concatenate(outs, axis=1))
    o = rows[0] if nb == 1 else jnp.concatenate(rows, axis=0)
    o_ref[...] = x + _dot(o, wxo_ref[...])


def _post(l, x, fox, mla, wo, g, wxq, mk, mv, wxo, *, b, tm, nb):
    n, d = x.shape
    s = n // b
    tiles = s // tm
    rows = nb * tm
    tok = lambda w: pl.BlockSpec((rows, w), lambda i, j: (i * tiles + j, 0))
    full = lambda a: pl.BlockSpec(a.shape, lambda i, j: (0,) * a.ndim)
    mem = pl.BlockSpec((1, nb) + mk.shape[2:], lambda i, j: (l, i) + (0,) * (mk.ndim - 2))
    dense = [pltpu.VMEM((mk.shape[2], d), mk.dtype)] * 2 if mk.ndim == 5 else []
    return pl.pallas_call(
        functools.partial(_post_kernel, nb=nb),
        grid=(b // nb, tiles),
        in_specs=[tok(d), tok(FOX_WIDTH), tok(MLA_WIDTH), full(wo), full(g), full(wxq), mem, mem, full(wxo)],
        out_specs=tok(d),
        out_shape=jax.ShapeDtypeStruct((n, d), F32),
        scratch_shapes=dense,
        compiler_params=_cparams(("parallel", "arbitrary")),
        name="post",
    )(x, fox, mla, wo, g, wxq, mk, mv, wxo)


def _mlp_kernel(x_ref, g_ref, wup_ref, wdn_ref, gf_ref, o_ref, *, final, ff_blk):
    x = x_ref[...]
    hm = _rms(x, g_ref[...]).astype(BF16)
    acc = x
    for c in range(D_FF // ff_blk):
        u = jnp.maximum(_dot(hm, wup_ref[:, c * ff_blk:(c + 1) * ff_blk]), 0.0)
        acc = acc + _dot(jnp.square(u).astype(BF16), wdn_ref[c * ff_blk:(c + 1) * ff_blk, :])
    o_ref[...] = _rms(acc, gf_ref[...]) if final else acc


def _mlp(x, g, wup, wdn, gf, *, final, tm):
    n, d = x.shape
    full = lambda a: pl.BlockSpec(a.shape, lambda i: (0,) * a.ndim, pipeline_mode=pl.Buffered(1))
    return pl.pallas_call(
        functools.partial(_mlp_kernel, final=final, ff_blk=1024),
        grid=(n // tm,),
        in_specs=[pl.BlockSpec((tm, d), lambda i: (i, 0)), full(g), full(wup), full(wdn), full(gf)],
        out_specs=pl.BlockSpec((tm, d), lambda i: (i, 0)),
        out_shape=jax.ShapeDtypeStruct((n, d), F32),
        compiler_params=_cparams(("parallel",)),
        name="mlp",
    )(x, g, wup, wdn, gf)


def _gate_block(a):
    pad = jnp.zeros(a.shape[:-1] + (LANES - FOX_HEADS * BIAS_SLOTS - FOX_HEADS,), a.dtype)
    return jnp.concatenate([jnp.repeat(a, BIAS_SLOTS, axis=-1), a, pad], axis=-1)


def _split_win(w):
    o = 3 * FOX_WIDTH
    o2 = o + FOX_HEADS + MLA_Q_RANK
    half = MLA_ROPE_DIM // 2
    kr = w[:, o2 + MLA_KV_RANK:]
    return dict(q=w[:, :FOX_WIDTH], k=w[:, FOX_WIDTH:2 * FOX_WIDTH], v=w[:, 2 * FOX_WIDTH:o], g=w[:, o:o + FOX_HEADS],
                cq=w[:, o + FOX_HEADS:o2], ckv=w[:, o2:o2 + MLA_KV_RANK], kr=kr,
                kr_sw=jnp.concatenate([kr[:, half:], kr[:, :half]], axis=1))


def _layout_win_sample(w):
    p = _split_win(w)
    return jnp.concatenate([p["q"], p["k"], p["v"], p["cq"], p["ckv"], p["kr"], p["kr_sw"], _gate_block(p["g"])], axis=1).astype(BF16)


def _layout_win_prompt(w):
    p = _split_win(w)
    wr = jnp.concatenate([p["k"], p["ckv"], p["kr"], p["kr_sw"], _gate_block(p["g"])], axis=1).astype(BF16)
    wt = jnp.concatenate([p["q"], p["v"], p["cq"], p["ckv"], p["kr"]], axis=1).T.astype(BF16)
    return wr, wt


def _layout_wuq_sample(w):
    w = w.reshape(MLA_Q_RANK, MLA_HEADS, MLA_NOPE_DIM + MLA_ROPE_DIM)
    nope = w[:, :, :MLA_NOPE_DIM].reshape(MLA_Q_RANK, MLA_HEADS * MLA_NOPE_DIM)
    r = w[:, :, MLA_NOPE_DIM:]
    half = MLA_ROPE_DIM // 2
    r2 = jnp.concatenate([r, r[:, :, half:], r[:, :, :half]], axis=2).reshape(MLA_Q_RANK, MLA_HEADS * LANES)
    return jnp.concatenate([nope, r2], axis=1).astype(BF16)


def _layout_wuq_prompt(w):
    w = w.reshape(MLA_Q_RANK, MLA_HEADS, MLA_NOPE_DIM + MLA_ROPE_DIM)
    return jnp.concatenate([w[:, :, :MLA_NOPE_DIM].reshape(MLA_Q_RANK, -1),
                            w[:, :, MLA_NOPE_DIM:].reshape(MLA_Q_RANK, -1)], axis=1).T.astype(BF16)


def _layout_wukv(w):
    w = w.reshape(MLA_KV_RANK, MLA_HEADS, MLA_NOPE_DIM + MLA_V_DIM)
    return jnp.concatenate([w[:, :, :MLA_NOPE_DIM].reshape(MLA_KV_RANK, -1),
                            w[:, :, MLA_NOPE_DIM:].reshape(MLA_KV_RANK, -1)], axis=1).astype(BF16)


def _rope_tables(pos):
    half = MLA_ROPE_DIM // 2
    inv = ROPE_THETA ** (-jnp.arange(half, dtype=F32) / half)
    ang = pos.astype(F32)[:, None] * inv[None, :]
    return jnp.cos(ang), jnp.sin(ang)


def _tile(n, pref):
    return pref if n % pref == 0 else n


def kernel(x_prompt, x_sample, mem_prompt, cache_fox_k, cache_fox_v, cache_fox_logf, cache_mla_ckv, cache_mla_krope,
           cache_mem_k, cache_mem_v, norm_mix, w_in, b_forget, mla_q_norm, w_uq, mla_kv_norm, w_ukv, w_out,
           norm_cross, norm_mem, w_xq, w_mk, w_mv, w_xo, norm_mlp, w_up, w_down, norm_final):
    depth = w_in.shape[0]
    bp, sp, d = x_prompt.shape
    bs, ss, _ = x_sample.shape
    past = cache_fox_k.shape[2]
    n_mem = mem_prompt.shape[1]
    row = lambda a: a.reshape(1, -1)
    col = lambda a: a.reshape(-1, 1)

    cos_p, sin_p = _rope_tables(jnp.arange(sp))
    cos_s, sin_s = _rope_tables(past + jnp.arange(ss))
    cs_p = jnp.concatenate([cos_p, cos_p, -sin_p, sin_p], axis=1)
    cs_s = jnp.tile(jnp.concatenate([cos_s, cos_s, -sin_s, sin_s], axis=1), (bs, 1))
    cos_pt, sin_pt = cos_p.T, sin_p.T

    mk_all, mv_all, mkb_all, mvb_all = _memory_kv(mem_prompt, norm_mem[:, None, :], w_mk.astype(BF16), w_mv.astype(BF16))

    kct = jnp.transpose(cache_fox_k, (0, 1, 3, 4, 2)).reshape(depth, bs, FOX_WIDTH, past)
    vct = jnp.transpose(cache_fox_v, (0, 1, 3, 4, 2)).reshape(depth, bs, FOX_WIDTH, past)
    krct = jnp.transpose(cache_mla_krope, (0, 1, 3, 2))
    kbct, tot = _cache_scan(jnp.transpose(cache_fox_logf, (0, 1, 3, 2)).astype(F32))
    cin_all = _gate_block(tot[..., 0])[:, :, None, :]

    cmk = cache_mem_k.reshape(depth, bs, n_mem, d)
    cmv = cache_mem_v.reshape(depth, bs, n_mem, d)

    tm_p = _tile(sp, 512)
    xp, xs = x_prompt, x_sample
    rows_p, rows_s = None, []
    for l in range(depth):
        wsplit_s = _layout_win_sample(w_in[l])
        wr, wt = _layout_win_prompt(w_in[l])
        bblk = _gate_block(b_forget[l][None, :])
        wukv = _layout_wukv(w_ukv[l])
        wnope, wvt = wukv[:, :MLA_HEADS * MLA_NOPE_DIM], wukv[:, MLA_HEADS * MLA_NOPE_DIM:].T
        wo, wxq, wxo = w_out[l].astype(BF16), w_xq[l].astype(BF16), w_xo[l].astype(BF16)
        wup, wdn = w_up[l].astype(BF16), w_down[l].astype(BF16)
        gmix, gq, gkv, gx, gm = row(norm_mix[l]), row(mla_q_norm[l]), row(mla_kv_norm[l]), row(norm_cross[l]), row(norm_mlp[l])
        final = l == depth - 1

        *rows_p, qt, k2, vt2, qmt, km2, vmt = _inproj_prompt(
            xp, cs_p, cos_pt, sin_pt, gmix, wr, wt, bblk, col(mla_q_norm[l]), _layout_wuq_prompt(w_uq[l]),
            gkv, col(mla_kv_norm[l]), wnope, wvt, rows_p, tm=tm_p)
        fox = _attention(qt, k2, vt2, n_chain=4, n_sub=2, chunk=1, v_rows=FOX_V_ROWS, v_dim=FOX_HEAD_DIM)
        mla = _attention(qmt, km2, vmt, n_chain=4, n_sub=1, chunk=CHUNK, v_rows=MLA_V_ROWS, v_dim=MLA_V_DIM)
        xp = _post(l, xp.reshape(bp * sp, d), fox.reshape(bp * sp, FOX_WIDTH), mla.reshape(bp * sp, MLA_WIDTH),
                   wo, gx, wxq, mkb_all, mvb_all, wxo, b=bp, tm=_tile(sp, 1024), nb=1)
        xp = _mlp(xp, gm, wup, wdn, row(norm_final), final=final, tm=_tile(bp * sp, 512)).reshape(bp, sp, d)

        ns = bs * ss
        outs = _inproj(xs.reshape(1, ns, d), cin_all[l], cs_s, gmix, wsplit_s, bblk, gq, _layout_wuq_sample(w_uq[l]),
                       gkv, wukv, tm=_tile(ns, 8 * ss), seg=ss)
        kf, vf, lf, ckv, kr, qf2, kf2, vf2, qm2, km2, vm2 = (o.reshape(bs, ss, o.shape[-1]) for o in outs)
        fox, mla = _sample_attention(l, kct, vct, kbct, cache_mla_ckv, krct, wukv.T, qf2, kf2, vf2, qm2, km2, vm2, chunk=CHUNK)
        xs = _post(l, xs.reshape(ns, d), fox.reshape(ns, FOX_WIDTH), mla.reshape(ns, MLA_WIDTH),
                   wo, gx, wxq, cache_mem_k, cache_mem_v, wxo, b=bs, tm=ss, nb=4 if bs % 4 == 0 else 1)
        xs = _mlp(xs, gm, wup, wdn, row(norm_final), final=final, tm=_tile(ns, 512)).reshape(bs, ss, d)
        rows_s.append((kf, vf, lf, ckv, kr))

    st = lambda rows, i: jnp.stack([r[i] for r in rows])
    heads5 = lambda a: jnp.transpose(a.reshape(depth, bp, FOX_HEADS, FOX_HEAD_DIM, sp), (0, 1, 4, 2, 3))
    lo = FOX_HEADS * BIAS_SLOTS
    kt_all, vt_all, lft_all, ckv_all, krt_all = rows_p
    return (xp, xs,
            heads5(kt_all), heads5(vt_all), jnp.transpose(lft_all, (0, 1, 3, 2)), ckv_all,
            jnp.transpose(krt_all, (0, 1, 3, 2)),
            mk_all, mv_all,
            st(rows_s, 0).reshape(depth, bs, ss, FOX_HEADS, FOX_HEAD_DIM), st(rows_s, 1).reshape(depth, bs, ss, FOX_HEADS, FOX_HEAD_DIM),
            st(rows_s, 2)[..., lo:lo + FOX_HEADS], st(rows_s, 3), st(rows_s, 4))
```

```python
import functools

import jax
import jax.numpy as jnp
from jax import lax
from jax.experimental import pallas as pl
from jax.experimental.pallas import tpu as pltpu

F32 = jnp.float32
BF16 = jnp.bfloat16

D_MODEL = 1024
CHUNK = 64
FOX_HEADS = 8
FOX_HEAD_DIM = 64
FOX_WIDTH = FOX_HEADS * FOX_HEAD_DIM
MLA_HEADS = 4
MLA_NOPE_DIM = 128
MLA_ROPE_DIM = 64
MLA_V_DIM = 128
MLA_WIDTH = MLA_HEADS * MLA_V_DIM
MLA_Q_RANK = 256
MLA_KV_RANK = 128
ROPE_THETA = 10000.0
X_HEADS = 4
X_HEAD_DIM = 256
D_FF = 4 * D_MODEL
EPS = 1e-6

LOG2E = 1.4426950408889634
LANES = 128
SUBLANES = 8
BF16_ROWS = 16
PAIR = 2 * LANES
BIAS_SLOTS = 8
VMEM_LIMIT = 56 * 1024 * 1024
FOX_V_ROWS = FOX_HEAD_DIM + BF16_ROWS
MLA_V_ROWS = MLA_V_DIM + BF16_ROWS

_OQ, _OK, _OV = 0, FOX_WIDTH, 2 * FOX_WIDTH
_OCQ = 3 * FOX_WIDTH
_OCKV = _OCQ + MLA_Q_RANK
_OKR = _OCKV + MLA_KV_RANK
_OG = _OKR + LANES
_RK, _RCKV, _RKR, _RG = 0, FOX_WIDTH, FOX_WIDTH + MLA_KV_RANK, FOX_WIDTH + MLA_KV_RANK + LANES
_TQ, _TV = 0, FOX_WIDTH
_TCQ = 2 * FOX_WIDTH
_TCKV = _TCQ + MLA_Q_RANK
_TKR = _TCKV + MLA_KV_RANK


def _cparams(sem):
    return pltpu.CompilerParams(dimension_semantics=sem, vmem_limit_bytes=VMEM_LIMIT)


def _rms(x, g):
    return x * lax.rsqrt(jnp.mean(jnp.square(x), axis=-1, keepdims=True) + EPS) * g


def _rms_rows(x, g):
    return x * lax.rsqrt(jnp.mean(jnp.square(x), axis=0, keepdims=True) + EPS) * g


def _dot(a, b):
    return jnp.dot(a, b, preferred_element_type=F32)


def _dot_nt(a, b):
    return lax.dot_general(a, b, (((1,), (1,)), ((), ())), preferred_element_type=F32)


def _split3(x):
    x1 = x.astype(BF16)
    r1 = x - x1.astype(F32)
    x2 = r1.astype(BF16)
    r2 = r1 - x2.astype(F32)
    return x1, x2, r2.astype(BF16)


def _cumsum_rows(x, carry):
    n = x.shape[0]
    blk = min(n, 256)
    r = lax.broadcasted_iota(jnp.int32, (blk, blk), 0)
    c = lax.broadcasted_iota(jnp.int32, (blk, blk), 1)
    tri = (c <= r).astype(BF16)
    outs = []
    for i in range(n // blk):
        x1, x2, x3 = _split3(x[i * blk:(i + 1) * blk])
        cb = _dot(tri, x1) + _dot(tri, x2) + _dot(tri, x3) + carry
        carry = cb[blk - 1:blk]
        outs.append(cb)
    return outs[0] if len(outs) == 1 else jnp.concatenate(outs, axis=0)


def _cumsum_lanes(x):
    n = x.shape[1]
    blk = min(n, 256)
    r = lax.broadcasted_iota(jnp.int32, (blk, blk), 0)
    c = lax.broadcasted_iota(jnp.int32, (blk, blk), 1)
    tri = (r <= c).astype(BF16)
    carry = jnp.zeros((x.shape[0], 1), F32)
    outs = []
    for i in range(n // blk):
        x1, x2, x3 = _split3(x[:, i * blk:(i + 1) * blk])
        cb = _dot(x1, tri) + _dot(x2, tri) + _dot(x3, tri) + carry
        carry = cb[:, blk - 1:blk]
        outs.append(cb)
    return outs[0] if len(outs) == 1 else jnp.concatenate(outs, axis=1)


def _bias_blocks(cum, slot_axis):
    c1, c2, c3 = (c.astype(F32) for c in _split3(cum))
    slot = lax.broadcasted_iota(jnp.int32, cum.shape, slot_axis)
    j = slot % BIAS_SLOTS
    valid = slot < FOX_HEADS * BIAS_SLOTS
    one = jnp.where(valid & (j < 6), 1.0, 0.0)
    qb = jnp.where(j == 0, c1, jnp.where(j == 1, c2, jnp.where(j == 2, c3, one)))
    kb = jnp.where(j < 3, one, jnp.where(j == 3, -c1, jnp.where(j == 4, -c2, jnp.where(j == 5, -c3, 0.0))))
    return jnp.where(valid, qb, 0.0), jnp.where(valid, kb, 0.0)


def _log_sigmoid(x):
    return jnp.minimum(x, 0.0) - jnp.log1p(jnp.exp(-jnp.abs(x)))


def _rope_pair(t, cs):
    u = t * cs
    return u + pltpu.roll(u, MLA_ROPE_DIM, 1)


def _rope_rows(x, cos, sin):
    half = MLA_ROPE_DIM // 2
    x1, x2 = x[:half], x[half:]
    return jnp.concatenate([x1 * cos - x2 * sin, x2 * cos + x1 * sin], axis=0)


def _inproj_kernel(x_ref, cin_ref, cs_ref, gmix_ref, win_ref, bblk_ref, gq_ref, wuq_ref, gkv_ref, wukv_ref,
                   kf_ref, vf_ref, lf_ref, ckv_ref, kr_ref,
                   qf2_ref, kf2_ref, vf2_ref, qm2_ref, km2_ref, vm2_ref, *, seg):
    x = x_ref[0]
    tm = x.shape[0]
    h = _rms(x, gmix_ref[...])
    z = _dot(h.astype(BF16), win_ref[...])
    cs = cs_ref[...]
    lane = lax.broadcasted_iota(jnp.int32, (1, LANES), 1)
    low = lane < MLA_ROPE_DIM
    ones_blk = jnp.ones((tm, LANES), BF16)

    kf_ref[0] = z[:, _OK:_OK + FOX_WIDTH]
    vf_ref[0] = z[:, _OV:_OV + FOX_WIDTH]

    lf = _log_sigmoid(z[:, _OG:_OG + LANES] + bblk_ref[...])
    lf_ref[0] = lf
    cums = [_cumsum_rows(lf[i * seg:(i + 1) * seg], cin_ref[i]) for i in range(tm // seg)]
    cum = cums[0] if len(cums) == 1 else jnp.concatenate(cums, axis=0)
    qb, kb = _bias_blocks(cum, 1)

    fox_scale = FOX_HEAD_DIM ** -0.5
    for p in range(FOX_HEADS // 2):
        lo, hi = p * PAIR, p * PAIR + LANES
        qf2_ref[0, :, lo:hi] = (z[:, _OQ + p * LANES:_OQ + (p + 1) * LANES] * fox_scale).astype(BF16)
        qf2_ref[0, :, hi:hi + LANES] = qb.astype(BF16)
        kf2_ref[0, :, lo:hi] = z[:, _OK + p * LANES:_OK + (p + 1) * LANES].astype(BF16)
        kf2_ref[0, :, hi:hi + LANES] = kb.astype(BF16)
        vf2_ref[0, :, lo:hi] = z[:, _OV + p * LANES:_OV + (p + 1) * LANES].astype(BF16)
        vf2_ref[0, :, hi:hi + LANES] = ones_blk

    kr = _rope_pair(z[:, _OKR:_OKR + LANES], cs)
    kr_ref[0] = kr[:, :MLA_ROPE_DIM]
    krb = jnp.where(low, kr, 0.0).astype(BF16)

    ckv = _rms(z[:, _OCKV:_OCKV + MLA_KV_RANK], gkv_ref[...])
    ckv_ref[0] = ckv
    kv = _dot(ckv.astype(BF16), wukv_ref[...])
    cqn = _rms(z[:, _OCQ:_OCQ + MLA_Q_RANK], gq_ref[...])
    qm = _dot(cqn.astype(BF16), wuq_ref[...])
    mla_scale = (MLA_NOPE_DIM + MLA_ROPE_DIM) ** -0.5
    for hd in range(MLA_HEADS):
        lo, hi = hd * PAIR, hd * PAIR + LANES
        qr = _rope_pair(qm[:, MLA_HEADS * LANES + hd * LANES:MLA_HEADS * LANES + (hd + 1) * LANES], cs)
        qm2_ref[0, :, lo:hi] = (qm[:, hd * LANES:(hd + 1) * LANES] * mla_scale).astype(BF16)
        qm2_ref[0, :, hi:hi + LANES] = (jnp.where(low, qr, 0.0) * mla_scale).astype(BF16)
        km2_ref[0, :, lo:hi] = kv[:, hd * LANES:(hd + 1) * LANES].astype(BF16)
        km2_ref[0, :, hi:hi + LANES] = krb
        vm2_ref[0, :, lo:hi] = kv[:, MLA_HEADS * LANES + hd * LANES:MLA_HEADS * LANES + (hd + 1) * LANES].astype(BF16)
        vm2_ref[0, :, hi:hi + LANES] = ones_blk


def _inproj(x, cin, cs, gmix, win, bblk, gq, wuq, gkv, wukv, *, tm, seg):
    b, s, _ = x.shape
    grid = (b, s // tm)
    tok = lambda w: pl.BlockSpec((1, tm, w), lambda i, j: (i, j, 0))
    full = lambda a: pl.BlockSpec(a.shape, lambda i, j: (0,) * a.ndim)
    out_shape = (
        jax.ShapeDtypeStruct((b, s, FOX_WIDTH), F32), jax.ShapeDtypeStruct((b, s, FOX_WIDTH), F32),
        jax.ShapeDtypeStruct((b, s, LANES), F32), jax.ShapeDtypeStruct((b, s, MLA_KV_RANK), F32),
        jax.ShapeDtypeStruct((b, s, MLA_ROPE_DIM), F32),
    ) + tuple(jax.ShapeDtypeStruct((b, s, 4 * PAIR), BF16) for _ in range(6))
    out_specs = (tok(FOX_WIDTH), tok(FOX_WIDTH), tok(LANES), tok(MLA_KV_RANK), tok(MLA_ROPE_DIM)) + (tok(4 * PAIR),) * 6
    return pl.pallas_call(
        functools.partial(_inproj_kernel, seg=seg),
        grid=grid,
        in_specs=[tok(D_MODEL), pl.BlockSpec((tm // seg, 1, LANES), lambda i, j: (j, 0, 0)),
                  pl.BlockSpec((tm, LANES), lambda i, j: (j, 0)),
                  full(gmix), full(win), full(bblk), full(gq), full(wuq), full(gkv), full(wukv)],
        out_specs=out_specs,
        out_shape=out_shape,
        compiler_params=_cparams(("parallel", "parallel")),
        name="inproj_sample",
    )(x, cin, cs, gmix, win, bblk, gq, wuq, gkv, wukv)


def _inproj_prompt_kernel(*refs, n_prev):
    (x_ref, cs_ref, cos_ref, sin_ref, gmix_ref, wr_ref, wt_ref, bblk_ref, gq_ref, wuqt_ref, gkv_ref, gkvc_ref,
     wnope_ref, wvt_ref) = refs[:14]
    prev_refs = refs[14:19] if n_prev else ()
    (kt_ref, vt_ref, lft_ref, ckv_ref, krt_ref,
     qt_ref, k2_ref, vt2_ref, qmt_ref, km2_ref, vmt_ref, carry_ref) = refs[14 + len(prev_refs):]

    @pl.when(pl.program_id(1) == 0)
    def _():
        carry_ref[...] = jnp.zeros_like(carry_ref)

    x = x_ref[0]
    tm = x.shape[0]
    hb = _rms(x, gmix_ref[...]).astype(BF16)
    zr = _dot(hb, wr_ref[...])
    zt = _dot_nt(wt_ref[...], hb)
    cs, cos, sin = cs_ref[...], cos_ref[...], sin_ref[...]
    ones_rows = jnp.ones((BF16_ROWS, tm), BF16)

    for prev_ref, out_ref in zip(prev_refs, (kt_ref, vt_ref, lft_ref, ckv_ref, krt_ref)):
        out_ref[:n_prev, 0] = prev_ref[:, 0]
    kt_ref[n_prev, 0] = zr[:, _RK:_RK + FOX_WIDTH].T
    vt_ref[n_prev, 0] = zt[_TV:_TV + FOX_WIDTH]

    lf = _log_sigmoid(zr[:, _RG:_RG + LANES] + bblk_ref[...])
    cum = _cumsum_rows(lf, carry_ref[...])
    carry_ref[...] = cum[tm - 1:tm]
    qb, kb = _bias_blocks(cum * LOG2E, 1)
    kb = kb.astype(BF16)
    qbt = qb.T
    lo = FOX_HEADS * BIAS_SLOTS
    lft_ref[n_prev, 0] = lf.T[lo:lo + FOX_HEADS]

    fox_scale = FOX_HEAD_DIM ** -0.5 * LOG2E
    rowi = lax.broadcasted_iota(jnp.int32, (LANES, tm), 0)
    for p in range(FOX_HEADS // 2):
        zq = zt[_TQ + p * LANES:_TQ + (p + 1) * LANES] * fox_scale
        for sub in range(2):
            hd = 2 * p + sub
            qt_ref[0, 0, hd * PAIR:hd * PAIR + LANES] = jnp.where(rowi // FOX_HEAD_DIM == sub, zq, 0.0).astype(BF16)
            qt_ref[0, 0, hd * PAIR + LANES:(hd + 1) * PAIR] = jnp.where(rowi // BIAS_SLOTS == hd, qbt, 0.0).astype(BF16)
        k2_ref[0, :, p * PAIR:p * PAIR + LANES] = zr[:, _RK + p * LANES:_RK + (p + 1) * LANES].astype(BF16)
        k2_ref[0, :, p * PAIR + LANES:(p + 1) * PAIR] = kb
    for hd in range(FOX_HEADS):
        r0 = hd * FOX_V_ROWS
        vt2_ref[0, 0, r0:r0 + FOX_HEAD_DIM] = zt[_TV + hd * FOX_HEAD_DIM:_TV + (hd + 1) * FOX_HEAD_DIM].astype(BF16)
        vt2_ref[0, 0, r0 + FOX_HEAD_DIM:r0 + FOX_V_ROWS] = ones_rows

    krt_ref[n_prev, 0] = _rope_rows(zt[_TKR:_TKR + MLA_ROPE_DIM], cos, sin)
    kr = _rope_pair(zr[:, _RKR:_RKR + LANES], cs)
    krb = jnp.where(lax.broadcasted_iota(jnp.int32, (1, LANES), 1) < MLA_ROPE_DIM, kr, 0.0).astype(BF16)
    ckv = _rms(zr[:, _RCKV:_RCKV + MLA_KV_RANK], gkv_ref[...])
    ckv_ref[n_prev, 0] = ckv
    knope = _dot(ckv.astype(BF16), wnope_ref[...])
    ckvt = _rms_rows(zt[_TCKV:_TCKV + MLA_KV_RANK], gkvc_ref[...])
    vmt = _dot(wvt_ref[...], ckvt.astype(BF16))
    cqnt = _rms_rows(zt[_TCQ:_TCQ + MLA_Q_RANK], gq_ref[...])
    qmt = _dot(wuqt_ref[...], cqnt.astype(BF16))
    mla_scale = (MLA_NOPE_DIM + MLA_ROPE_DIM) ** -0.5 * LOG2E
    zeros_rows = jnp.zeros((PAIR - MLA_NOPE_DIM - MLA_ROPE_DIM, tm), BF16)
    for hd in range(MLA_HEADS):
        q0 = hd * PAIR
        qmt_ref[0, 0, q0:q0 + MLA_NOPE_DIM] = (qmt[hd * MLA_NOPE_DIM:(hd + 1) * MLA_NOPE_DIM] * mla_scale).astype(BF16)
        r0 = MLA_HEADS * MLA_NOPE_DIM + hd * MLA_ROPE_DIM
        qmt_ref[0, 0, q0 + MLA_NOPE_DIM:q0 + MLA_NOPE_DIM + MLA_ROPE_DIM] = (
            _rope_rows(qmt[r0:r0 + MLA_ROPE_DIM], cos, sin) * mla_scale).astype(BF16)
        qmt_ref[0, 0, q0 + MLA_NOPE_DIM + MLA_ROPE_DIM:q0 + PAIR] = zeros_rows
        km2_ref[0, :, q0:q0 + LANES] = knope[:, hd * LANES:(hd + 1) * LANES].astype(BF16)
        km2_ref[0, :, q0 + LANES:q0 + PAIR] = krb
        v0 = hd * MLA_V_ROWS
        vmt_ref[0, 0, v0:v0 + MLA_V_DIM] = vmt[hd * MLA_V_DIM:(hd + 1) * MLA_V_DIM].astype(BF16)
        vmt_ref[0, 0, v0 + MLA_V_DIM:v0 + MLA_V_ROWS] = ones_rows


def _inproj_prompt(x, cs, cos_t, sin_t, gmix, wr, wt, bblk, gq_col, wuqt, gkv, gkv_col, wnope, wvt, prev, *, tm):
    b, s, _ = x.shape
    nt = s // tm
    n_prev = 0 if prev is None else prev[0].shape[0]
    full = lambda a: pl.BlockSpec(a.shape, lambda i, j: (0,) * a.ndim, pipeline_mode=pl.Buffered(1))
    tile4 = lambda rows: pl.BlockSpec((1, 1, rows, tm), lambda i, j: (i, j, 0, 0))
    tok = lambda w: pl.BlockSpec((1, tm, w), lambda i, j: (i, j, 0))
    feat = lambda n, rows: pl.BlockSpec((n, 1, rows, tm), lambda i, j: (0, i, 0, j))
    toks = lambda n, w: pl.BlockSpec((n, 1, tm, w), lambda i, j: (0, i, j, 0))
    row_specs = lambda n: [feat(n, FOX_WIDTH), feat(n, FOX_WIDTH), feat(n, FOX_HEADS), toks(n, MLA_KV_RANK),
                           feat(n, MLA_ROPE_DIM)]
    ins = [x, cs, cos_t, sin_t, gmix, wr, wt, bblk, gq_col, wuqt, gkv, gkv_col, wnope, wvt]
    in_specs = [tok(D_MODEL), pl.BlockSpec((tm, LANES), lambda i, j: (j, 0)),
                pl.BlockSpec((cos_t.shape[0], tm), lambda i, j: (0, j)),
                pl.BlockSpec((sin_t.shape[0], tm), lambda i, j: (0, j))] + [full(a) for a in ins[4:]]
    if n_prev:
        ins += list(prev)
        in_specs += row_specs(n_prev)
    op4 = lambda rows: jax.ShapeDtypeStruct((b, nt, rows, tm), BF16)
    nl = n_prev + 1
    out_shape = (jax.ShapeDtypeStruct((nl, b, FOX_WIDTH, s), F32),) * 2 + (
        jax.ShapeDtypeStruct((nl, b, FOX_HEADS, s), F32), jax.ShapeDtypeStruct((nl, b, s, MLA_KV_RANK), F32),
        jax.ShapeDtypeStruct((nl, b, MLA_ROPE_DIM, s), F32),
        op4(FOX_HEADS * PAIR), jax.ShapeDtypeStruct((b, s, 4 * PAIR), BF16), op4(FOX_HEADS * FOX_V_ROWS),
        op4(MLA_HEADS * PAIR), jax.ShapeDtypeStruct((b, s, 4 * PAIR), BF16), op4(MLA_HEADS * MLA_V_ROWS))
    out_specs = tuple(row_specs(nl)) + (
        tile4(FOX_HEADS * PAIR), tok(4 * PAIR), tile4(FOX_HEADS * FOX_V_ROWS),
        tile4(MLA_HEADS * PAIR), tok(4 * PAIR), tile4(MLA_HEADS * MLA_V_ROWS))
    return pl.pallas_call(
        functools.partial(_inproj_prompt_kernel, n_prev=n_prev),
        grid=(b, nt),
        in_specs=in_specs,
        out_specs=out_specs,
        out_shape=out_shape,
        scratch_shapes=[pltpu.VMEM((1, LANES), F32)],
        compiler_params=_cparams(("parallel", "arbitrary")),
        name="inproj_prompt",
    )(*ins)


def _attn_kernel(qt_ref, k_ref, vt_ref, o_ref, sa_ref, sb_ref, sd_ref, m_ref, acc_ref, *,
                 t, nt, n_chain, n_sub, chunk, v_rows, v_dim):
    key = lax.broadcasted_iota(jnp.int32, (t, t), 0)
    qry = lax.broadcasted_iota(jnp.int32, (t, t), 1)
    allowed = (key // chunk) <= (qry // chunk)

    def scores(qblk, kblk, dst_ref, diagonal=False):
        off = pl.multiple_of(kblk * t, t)
        for c in range(n_chain):
            g = c // n_sub
            st = _dot(k_ref[0, pl.ds(off, t), g * PAIR:(g + 1) * PAIR], qt_ref[0, qblk, c * PAIR:(c + 1) * PAIR])
            if diagonal:
                st = jnp.where(allowed, st, -1e30)
            dst_ref[c, :t] = st
            dst_ref[c, t:] = jnp.broadcast_to(jnp.max(st, axis=0, keepdims=True), (SUBLANES, t))

    def consume(kblk, src_ref):
        for c in range(n_chain):
            st = src_ref[c, :t]
            m = m_ref[c]
            m_new = jnp.maximum(m, src_ref[c, t:t + 1])
            pt = jnp.exp2(st - m_new).astype(BF16)
            acc_ref[c] = jnp.exp2(m - m_new) * acc_ref[c] + _dot(vt_ref[0, kblk, c * v_rows:(c + 1) * v_rows], pt)
            m_ref[c] = m_new

    def next_diagonal(i):
        nxt = jnp.minimum(i + 1, nt - 1)
        scores(nxt, nxt, sd_ref, diagonal=True)

    def query_block(i, carry):
        m_ref[...] = jnp.full(m_ref.shape, -1e30, F32)
        acc_ref[...] = jnp.zeros(acc_ref.shape, F32)

        @pl.when(i == 0)
        def _():
            consume(0, sd_ref)
            next_diagonal(i)

        @pl.when(i > 0)
        def _():
            scores(i, 0, sa_ref)
            consume(i, sd_ref)
            n_pairs = (i - 1) // 2

            def pair(p, c):
                scores(i, 2 * p + 1, sb_ref)
                consume(2 * p, sa_ref)
                scores(i, 2 * p + 2, sa_ref)
                consume(2 * p + 1, sb_ref)
                return c

            lax.fori_loop(0, n_pairs, pair, 0)
            one_more = (i - 1) % 2 == 1

            @pl.when(one_more)
            def _():
                scores(i, i - 1, sb_ref)
                consume(2 * n_pairs, sa_ref)
                next_diagonal(i)
                consume(i - 1, sb_ref)

            @pl.when(jnp.logical_not(one_more))
            def _():
                next_diagonal(i)
                consume(2 * n_pairs, sa_ref)

        rows = pl.ds(pl.multiple_of(i * t, t), t)
        for g in range(n_chain // n_sub):
            outs = []
            for sub in range(n_sub):
                acc = acc_ref[g * n_sub + sub]
                outs.append(acc[:v_dim] / acc[v_dim:v_dim + 1])
            ot = outs[0] if n_sub == 1 else jnp.concatenate(outs, axis=0)
            o_ref[0, rows, g * LANES:(g + 1) * LANES] = ot.T.astype(o_ref.dtype)
        return carry

    scores(0, 0, sd_ref, diagonal=True)
    lax.fori_loop(0, nt, query_block, 0)


def _attention(qt, k2, vt, *, n_chain, n_sub, chunk, v_rows, v_dim):
    b, nt, rows, t = qt.shape
    s = nt * t
    steps = rows // (n_chain * PAIR)
    gps = n_chain // n_sub
    return pl.pallas_call(
        functools.partial(_attn_kernel, t=t, nt=nt, n_chain=n_chain, n_sub=n_sub, chunk=chunk, v_rows=v_rows, v_dim=v_dim),
        grid=(b, steps),
        in_specs=[pl.BlockSpec((1, nt, n_chain * PAIR, t), lambda bi, g: (bi, 0, g, 0)),
                  pl.BlockSpec((1, s, gps * PAIR), lambda bi, g: (bi, 0, g), pipeline_mode=pl.Buffered(1)),
                  pl.BlockSpec((1, nt, n_chain * v_rows, t), lambda bi, g: (bi, 0, g, 0),
                               pipeline_mode=pl.Buffered(1))],
        out_specs=pl.BlockSpec((1, s, gps * LANES), lambda bi, g: (bi, 0, g)),
        out_shape=jax.ShapeDtypeStruct((b, s, 4 * LANES), BF16),
        scratch_shapes=[pltpu.VMEM((n_chain, t + SUBLANES, t), F32)] * 3
                       + [pltpu.VMEM((n_chain, 1, t), F32), pltpu.VMEM((n_chain, v_rows, t), F32)],
        compiler_params=_cparams(("parallel", "arbitrary")),
        name="attn_fox" if n_sub > 1 else "attn_mla",
    )(qt, k2, vt)


def _cache_scan_kernel(lft_ref, kbct_ref, tot_ref):
    lf = lft_ref[0, 0]
    heads, n = lf.shape
    cum = _cumsum_lanes(jnp.concatenate([lf, jnp.zeros_like(lf)], axis=0))[:heads]
    tot_ref[0, 0] = jnp.broadcast_to(cum[:, n - 1:n], (heads, LANES))
    rep = jnp.concatenate([jnp.broadcast_to(cum[hd:hd + 1], (BIAS_SLOTS, n)) for hd in range(heads)]
                          + [jnp.zeros((LANES - heads * BIAS_SLOTS, n), F32)], axis=0)
    kbct_ref[0, 0] = _bias_blocks(rep, 0)[1].astype(BF16)


def _cache_scan(lft):
    depth, b, heads, n = lft.shape
    return pl.pallas_call(
        _cache_scan_kernel,
        grid=(depth, b),
        in_specs=[pl.BlockSpec((1, 1, heads, n), lambda l, i: (l, i, 0, 0))],
        out_specs=(pl.BlockSpec((1, 1, LANES, n), lambda l, i: (l, i, 0, 0)),
                   pl.BlockSpec((1, 1, heads, LANES), lambda l, i: (l, i, 0, 0))),
        out_shape=(jax.ShapeDtypeStruct((depth, b, LANES, n), BF16), jax.ShapeDtypeStruct((depth, b, heads, LANES), F32)),
        compiler_params=_cparams(("parallel", "parallel")),
        name="cache_scan",
    )(lft)


def _select_head(q2, g, sub):
    lane = lax.broadcasted_iota(jnp.int32, (1, PAIR), 1)
    payload = (lane < LANES) & (lane // FOX_HEAD_DIM == sub)
    side = (lane >= LANES) & ((lane - LANES) // BIAS_SLOTS == 2 * g + sub)
    return jnp.where(payload | side, q2.astype(F32), 0.0).astype(BF16)


def _softmax_pv(s_c, s_n, vt_c, v_n):
    m = jnp.maximum(jnp.max(s_c, axis=1, keepdims=True), jnp.max(s_n, axis=1, keepdims=True))
    p_c = jnp.exp(s_c - m)
    p_n = jnp.exp(s_n - m)
    den = jnp.sum(p_c, axis=1, keepdims=True) + jnp.sum(p_n, axis=1, keepdims=True)
    return (_dot_nt(p_c.astype(BF16), vt_c) + _dot(p_n.astype(BF16), v_n)) / den


def _sample_attn_kernel(kct_ref, vct_ref, kbct_ref, ckvc_ref, krct_ref, wukvt_ref,
                        qf2_ref, kf2_ref, vf2_ref, qm2_ref, km2_ref, vm2_ref,
                        fox_ref, mla_ref, *, past, chunk):
    n = qf2_ref.shape[1]
    row = lax.broadcasted_iota(jnp.int32, (n, n), 0)
    col = lax.broadcasted_iota(jnp.int32, (n, n), 1)
    chunk_ok = ((past + col) // chunk) <= ((past + row) // chunk)
    row2 = lax.broadcasted_iota(jnp.int32, (2 * n, n), 0) % n
    frame_ok2 = lax.broadcasted_iota(jnp.int32, (2 * n, n), 1) <= row2
    first = lax.broadcasted_iota(jnp.int32, (1, LANES), 1) < FOX_HEAD_DIM
    kbct = kbct_ref[0, 0]

    fox_scores = []
    for p in range(FOX_HEADS // 2):
        kkt = jnp.concatenate([kct_ref[0, 0, p * LANES:(p + 1) * LANES].astype(BF16), kbct], axis=0)
        qpair = qf2_ref[0, :, p * PAIR:(p + 1) * PAIR]
        q_st = jnp.concatenate([_select_head(qpair, p, sub) for sub in range(2)], axis=0)
        s_n = _dot_nt(q_st, kf2_ref[0, :, p * PAIR:(p + 1) * PAIR])
        fox_scores.append((_dot(q_st, kkt), jnp.where(frame_ok2, s_n, -1e30)))

    kvt_c = _dot_nt(wukvt_ref[...], ckvc_ref[0, 0].astype(BF16))
    krct = krct_ref[0, 0].astype(BF16)
    pad = jnp.zeros((PAIR - MLA_NOPE_DIM - MLA_ROPE_DIM, krct.shape[1]), BF16)
    mla_scores = []
    for hd in range(MLA_HEADS):
        kkt = jnp.concatenate([kvt_c[hd * LANES:(hd + 1) * LANES].astype(BF16), krct, pad], axis=0)
        q = qm2_ref[0, :, hd * PAIR:(hd + 1) * PAIR]
        s_n = jnp.where(chunk_ok, _dot_nt(q, km2_ref[0, :, hd * PAIR:(hd + 1) * PAIR]), -1e30)
        mla_scores.append((_dot(q, kkt), s_n))

    for p, (s_c, s_n) in enumerate(fox_scores):
        vvt = vct_ref[0, 0, p * LANES:(p + 1) * LANES].astype(BF16)
        o = _softmax_pv(s_c, s_n, vvt, vf2_ref[0, :, p * PAIR:p * PAIR + LANES])
        fox_ref[0, :, p * LANES:(p + 1) * LANES] = jnp.where(first, o[:n], o[n:]).astype(fox_ref.dtype)
    for hd, (s_c, s_n) in enumerate(mla_scores):
        vvt = kvt_c[MLA_HEADS * LANES + hd * LANES:MLA_HEADS * LANES + (hd + 1) * LANES].astype(BF16)
        o = _softmax_pv(s_c, s_n, vvt, vm2_ref[0, :, hd * PAIR:hd * PAIR + LANES])
        mla_ref[0, :, hd * LANES:(hd + 1) * LANES] = o.astype(mla_ref.dtype)


def _sample_attention(l, kct, vct, kbct, ckvc, krct, wukvt, qf2, kf2, vf2, qm2, km2, vm2, *, chunk):
    b, n = qf2.shape[:2]
    past = kct.shape[3]
    cache = lambda a: pl.BlockSpec((1, 1) + a.shape[2:], lambda i: (l, i, 0, 0))
    per_b = lambda a: pl.BlockSpec((1,) + a.shape[1:], lambda i: (i, 0, 0))
    return pl.pallas_call(
        functools.partial(_sample_attn_kernel, past=past, chunk=chunk),
        grid=(b,),
        in_specs=[cache(kct), cache(vct), cache(kbct), cache(ckvc), cache(krct),
                  pl.BlockSpec(wukvt.shape, lambda i: (0, 0)),
                  per_b(qf2), per_b(kf2), per_b(vf2), per_b(qm2), per_b(km2), per_b(vm2)],
        out_specs=(pl.BlockSpec((1, n, FOX_WIDTH), lambda i: (i, 0, 0)),
                   pl.BlockSpec((1, n, MLA_WIDTH), lambda i: (i, 0, 0))),
        out_shape=(jax.ShapeDtypeStruct((b, n, FOX_WIDTH), BF16), jax.ShapeDtypeStruct((b, n, MLA_WIDTH), BF16)),
        compiler_params=_cparams(("parallel",)),
        name="sample_attn",
    )(kct, vct, kbct, ckvc, krct, wukvt, qf2, kf2, vf2, qm2, km2, vm2)


def _memkv_kernel(mem_ref, g_ref, wk_ref, wv_ref, mk_ref, mv_ref, mkb_ref, mvb_ref):
    m = _rms(mem_ref[0], g_ref[0]).astype(BF16)
    mk = _dot(m, wk_ref[0])
    mv = _dot(m, wv_ref[0])
    for hd in range(X_HEADS):
        mk_ref[0, 0, :, hd, :] = mk[:, hd * X_HEAD_DIM:(hd + 1) * X_HEAD_DIM]
        mv_ref[0, 0, :, hd, :] = mv[:, hd * X_HEAD_DIM:(hd + 1) * X_HEAD_DIM]
    mkb_ref[0, 0] = mk.astype(BF16)
    mvb_ref[0, 0] = mv.astype(BF16)


def _memory_kv(mem, g, wk, wv):
    b, n, d = mem.shape
    depth = g.shape[0]
    o_spec = pl.BlockSpec((1, 1, n, d), lambda l, i: (l, i, 0, 0))
    o5_spec = pl.BlockSpec((1, 1, n, X_HEADS, X_HEAD_DIM), lambda l, i: (l, i, 0, 0, 0))
    w_spec = pl.BlockSpec((1, d, d), lambda l, i: (l, 0, 0))
    return pl.pallas_call(
        _memkv_kernel,
        grid=(depth, b),
        in_specs=[pl.BlockSpec((1, n, d), lambda l, i: (i, 0, 0)),
                  pl.BlockSpec((1, 1, d), lambda l, i: (l, 0, 0)), w_spec, w_spec],
        out_specs=(o5_spec, o5_spec, o_spec, o_spec),
        out_shape=(jax.ShapeDtypeStruct((depth, b, n, X_HEADS, X_HEAD_DIM), F32),) * 2
                  + (jax.ShapeDtypeStruct((depth, b, n, d), BF16),) * 2,
        compiler_params=_cparams(("arbitrary", "arbitrary")),
        name="memory_kv",
    )(mem, g, wk, wv)


def _post_kernel(x_ref, fox_ref, mla_ref, wo_ref, g_ref, wxq_ref, mk_ref, mv_ref, wxo_ref, o_ref, *dense, nb):
    x = x_ref[...] + _dot(fox_ref[...], wo_ref[:FOX_WIDTH, :]) + _dot(mla_ref[...], wo_ref[FOX_WIDTH:, :])
    hc = _rms(x, g_ref[...]).astype(BF16)
    q = (_dot(hc, wxq_ref[...]) * (X_HEAD_DIM ** -0.5)).astype(BF16)
    tm = x.shape[0] // nb
    rows = []
    for bi in range(nb):
        if dense:
            for src_ref, dst_ref in zip((mk_ref, mv_ref), dense):
                for hd in range(X_HEADS):
                    dst_ref[:, hd * X_HEAD_DIM:(hd + 1) * X_HEAD_DIM] = src_ref[0, bi, :, hd, :]
            mk_b, mv_b = dense
        else:
            mk_b, mv_b = mk_ref.at[0, bi], mv_ref.at[0, bi]
        outs = []
        for hd in range(X_HEADS):
            sl = slice(hd * X_HEAD_DIM, (hd + 1) * X_HEAD_DIM)
            s = _dot_nt(q[bi * tm:(bi + 1) * tm, sl], mk_b[:, sl].astype(BF16))
            p = jnp.exp(s - jnp.max(s, axis=1, keepdims=True))
            den = jnp.sum(p, axis=1, keepdims=True)
            outs.append((_dot(p.astype(BF16), mv_b[:, sl].astype(BF16)) / den).astype(BF16))
        rows.append(jnp.concatenate(outs, axis=1))
    o = rows[0] if nb == 1 else jnp.concatenate(rows, axis=0)
    o_ref[...] = x + _dot(o, wxo_ref[...])


def _post(l, x, fox, mla, wo, g, wxq, mk, mv, wxo, *, b, tm, nb):
    n, d = x.shape
    s = n // b
    tiles = s // tm
    rows = nb * tm
    tok = lambda w: pl.BlockSpec((rows, w), lambda i, j: (i * tiles + j, 0))
    full = lambda a: pl.BlockSpec(a.shape, lambda i, j: (0,) * a.ndim)
    mem = pl.BlockSpec((1, nb) + mk.shape[2:], lambda i, j: (l, i) + (0,) * (mk.ndim - 2))
    dense = [pltpu.VMEM((mk.shape[2], d), mk.dtype)] * 2 if mk.ndim == 5 else []
    return pl.pallas_call(
        functools.partial(_post_kernel, nb=nb),
        grid=(b // nb, tiles),
        in_specs=[tok(d), tok(FOX_WIDTH), tok(MLA_WIDTH), full(wo), full(g), full(wxq), mem, mem, full(wxo)],
        out_specs=tok(d),
        out_shape=jax.ShapeDtypeStruct((n, d), F32),
        scratch_shapes=dense,
        compiler_params=_cparams(("parallel", "arbitrary")),
        name="post",
    )(x, fox, mla, wo, g, wxq, mk, mv, wxo)


def _mlp_kernel(x_ref, g_ref, wup_ref, wdn_ref, gf_ref, o_ref, *, final, ff_blk):
    x = x_ref[...]
    hm = _rms(x, g_ref[...]).astype(BF16)
    acc = x
    for c in range(D_FF // ff_blk):
        u = jnp.maximum(_dot(hm, wup_ref[:, c * ff_blk:(c + 1) * ff_blk]), 0.0)
        acc = acc + _dot(jnp.square(u).astype(BF16), wdn_ref[c * ff_blk:(c + 1) * ff_blk, :])
    o_ref[...] = _rms(acc, gf_ref[...]) if final else acc


def _mlp(x, g, wup, wdn, gf, *, final, tm):
    n, d = x.shape
    full = lambda a: pl.BlockSpec(a.shape, lambda i: (0,) * a.ndim, pipeline_mode=pl.Buffered(1))
    return pl.pallas_call(
        functools.partial(_mlp_kernel, final=final, ff_blk=1024),
        grid=(n // tm,),
        in_specs=[pl.BlockSpec((tm, d), lambda i: (i, 0)), full(g), full(wup), full(wdn), full(gf)],
        out_specs=pl.BlockSpec((tm, d), lambda i: (i, 0)),
        out_shape=jax.ShapeDtypeStruct((n, d), F32),
        compiler_params=_cparams(("parallel",)),
        name="mlp",
    )(x, g, wup, wdn, gf)


def _gate_block(a):
    pad = jnp.zeros(a.shape[:-1] + (LANES - FOX_HEADS * BIAS_SLOTS - FOX_HEADS,), a.dtype)
    return jnp.concatenate([jnp.repeat(a, BIAS_SLOTS, axis=-1), a, pad], axis=-1)


def _split_win(w):
    o = 3 * FOX_WIDTH
    o2 = o + FOX_HEADS + MLA_Q_RANK
    half = MLA_ROPE_DIM // 2
    kr = w[:, o2 + MLA_KV_RANK:]
    return dict(q=w[:, :FOX_WIDTH], k=w[:, FOX_WIDTH:2 * FOX_WIDTH], v=w[:, 2 * FOX_WIDTH:o], g=w[:, o:o + FOX_HEADS],
                cq=w[:, o + FOX_HEADS:o2], ckv=w[:, o2:o2 + MLA_KV_RANK], kr=kr,
                kr_sw=jnp.concatenate([kr[:, half:], kr[:, :half]], axis=1))


def _layout_win_sample(w):
    p = _split_win(w)
    return jnp.concatenate([p["q"], p["k"], p["v"], p["cq"], p["ckv"], p["kr"], p["kr_sw"], _gate_block(p["g"])], axis=1).astype(BF16)


def _layout_win_prompt(w):
    p = _split_win(w)
    wr = jnp.concatenate([p["k"], p["ckv"], p["kr"], p["kr_sw"], _gate_block(p["g"])], axis=1).astype(BF16)
    wt = jnp.concatenate([p["q"], p["v"], p["cq"], p["ckv"], p["kr"]], axis=1).T.astype(BF16)
    return wr, wt


def _layout_wuq_sample(w):
    w = w.reshape(MLA_Q_RANK, MLA_HEADS, MLA_NOPE_DIM + MLA_ROPE_DIM)
    nope = w[:, :, :MLA_NOPE_DIM].reshape(MLA_Q_RANK, MLA_HEADS * MLA_NOPE_DIM)
    r = w[:, :, MLA_NOPE_DIM:]
    half = MLA_ROPE_DIM // 2
    r2 = jnp.concatenate([r, r[:, :, half:], r[:, :, :half]], axis=2).reshape(MLA_Q_RANK, MLA_HEADS * LANES)
    return jnp.concatenate([nope, r2], axis=1).astype(BF16)


def _layout_wuq_prompt(w):
    w = w.reshape(MLA_Q_RANK, MLA_HEADS, MLA_NOPE_DIM + MLA_ROPE_DIM)
    return jnp.concatenate([w[:, :, :MLA_NOPE_DIM].reshape(MLA_Q_RANK, -1),
                            w[:, :, MLA_NOPE_DIM:].reshape(MLA_Q_RANK, -1)], axis=1).T.astype(BF16)


def _layout_wukv(w):
    w = w.reshape(MLA_KV_RANK, MLA_HEADS, MLA_NOPE_DIM + MLA_V_DIM)
    return jnp.concatenate([w[:, :, :MLA_NOPE_DIM].reshape(MLA_KV_RANK, -1),
                            w[:, :, MLA_NOPE_DIM:].reshape(MLA_KV_RANK, -1)], axis=1).astype(BF16)


def _rope_tables(pos):
    half = MLA_ROPE_DIM // 2
    inv = ROPE_THETA ** (-jnp.arange(half, dtype=F32) / half)
    ang = pos.astype(F32)[:, None] * inv[None, :]
    return jnp.cos(ang), jnp.sin(ang)


def _tile(n, pref):
    return pref if n % pref == 0 else n


def kernel(x_prompt, x_sample, mem_prompt, cache_fox_k, cache_fox_v, cache_fox_logf, cache_mla_ckv, cache_mla_krope,
           cache_mem_k, cache_mem_v, norm_mix, w_in, b_forget, mla_q_norm, w_uq, mla_kv_norm, w_ukv, w_out,
           norm_cross, norm_mem, w_xq, w_mk, w_mv, w_xo, norm_mlp, w_up, w_down, norm_final):
    depth = w_in.shape[0]
    bp, sp, d = x_prompt.shape
    bs, ss, _ = x_sample.shape
    past = cache_fox_k.shape[2]
    n_mem = mem_prompt.shape[1]
    row = lambda a: a.reshape(1, -1)
    col = lambda a: a.reshape(-1, 1)

    cos_p, sin_p = _rope_tables(jnp.arange(sp))
    cos_s, sin_s = _rope_tables(past + jnp.arange(ss))
    cs_p = jnp.concatenate([cos_p, cos_p, -sin_p, sin_p], axis=1)
    cs_s = jnp.tile(jnp.concatenate([cos_s, cos_s, -sin_s, sin_s], axis=1), (bs, 1))
    cos_pt, sin_pt = cos_p.T, sin_p.T

    mk_all, mv_all, mkb_all, mvb_all = _memory_kv(mem_prompt, norm_mem[:, None, :], w_mk.astype(BF16), w_mv.astype(BF16))

    kct = jnp.transpose(cache_fox_k, (0, 1, 3, 4, 2)).reshape(depth, bs, FOX_WIDTH, past)
    vct = jnp.transpose(cache_fox_v, (0, 1, 3, 4, 2)).reshape(depth, bs, FOX_WIDTH, past)
    krct = jnp.transpose(cache_mla_krope, (0, 1, 3, 2))
    kbct, tot = _cache_scan(jnp.transpose(cache_fox_logf, (0, 1, 3, 2)).astype(F32))
    cin_all = _gate_block(tot[..., 0])[:, :, None, :]

    cmk = cache_mem_k.reshape(depth, bs, n_mem, d)
    cmv = cache_mem_v.reshape(depth, bs, n_mem, d)

    tm_p = _tile(sp, 512)
    xp, xs = x_prompt, x_sample
    rows_p, rows_s = None, []
    for l in range(depth):
        wsplit_s = _layout_win_sample(w_in[l])
        wr, wt = _layout_win_prompt(w_in[l])
        bblk = _gate_block(b_forget[l][None, :])
        wukv = _layout_wukv(w_ukv[l])
        wnope, wvt = wukv[:, :MLA_HEADS * MLA_NOPE_DIM], wukv[:, MLA_HEADS * MLA_NOPE_DIM:].T
        wo, wxq, wxo = w_out[l].astype(BF16), w_xq[l].astype(BF16), w_xo[l].astype(BF16)
        wup, wdn = w_up[l].astype(BF16), w_down[l].astype(BF16)
        gmix, gq, gkv, gx, gm = row(norm_mix[l]), row(mla_q_norm[l]), row(mla_kv_norm[l]), row(norm_cross[l]), row(norm_mlp[l])
        final = l == depth - 1

        *rows_p, qt, k2, vt2, qmt, km2, vmt = _inproj_prompt(
            xp, cs_p, cos_pt, sin_pt, gmix, wr, wt, bblk, col(mla_q_norm[l]), _layout_wuq_prompt(w_uq[l]),
            gkv, col(mla_kv_norm[l]), wnope, wvt, rows_p, tm=tm_p)
        fox = _attention(qt, k2, vt2, n_chain=4, n_sub=2, chunk=1, v_rows=FOX_V_ROWS, v_dim=FOX_HEAD_DIM)
        mla = _attention(qmt, km2, vmt, n_chain=4, n_sub=1, chunk=CHUNK, v_rows=MLA_V_ROWS, v_dim=MLA_V_DIM)
        xp = _post(l, xp.reshape(bp * sp, d), fox.reshape(bp * sp, FOX_WIDTH), mla.reshape(bp * sp, MLA_WIDTH),
                   wo, gx, wxq, mkb_all, mvb_all, wxo, b=bp, tm=_tile(sp, 1024), nb=1)
        xp = _mlp(xp, gm, wup, wdn, row(norm_final), final=final, tm=_tile(bp * sp, 512)).reshape(bp, sp, d)

        ns = bs * ss
        outs = _inproj(xs.reshape(1, ns, d), cin_all[l], cs_s, gmix, wsplit_s, bblk, gq, _layout_wuq_sample(w_uq[l]),
                       gkv, wukv, tm=_tile(ns, 8 * ss), seg=ss)
        kf, vf, lf, ckv, kr, qf2, kf2, vf2, qm2, km2, vm2 = (o.reshape(bs, ss, o.shape[-1]) for o in outs)
        fox, mla = _sample_attention(l, kct, vct, kbct, cache_mla_ckv, krct, wukv.T, qf2, kf2, vf2, qm2, km2, vm2, chunk=CHUNK)
        xs = _post(l, xs.reshape(ns, d), fox.reshape(ns, FOX_WIDTH), mla.reshape(ns, MLA_WIDTH),
                   wo, gx, wxq, cache_mem_k, cache_mem_v, wxo, b=bs, tm=ss, nb=4 if bs % 4 == 0 else 1)
        xs = _mlp(xs, gm, wup, wdn, row(norm_final), final=final, tm=_tile(ns, 512)).reshape(bs, ss, d)
        rows_s.append((kf, vf, lf, ckv, kr))

    st = lambda rows, i: jnp.stack([r[i] for r in rows])
    heads5 = lambda a: jnp.transpose(a.reshape(depth, bp, FOX_HEADS, FOX_HEAD_DIM, sp), (0, 1, 4, 2, 3))
    lo = FOX_HEADS * BIAS_SLOTS
    kt_all, vt_all, lft_all, ckv_all, krt_all = rows_p
    return (xp, xs,
            heads5(kt_all), heads5(vt_all), jnp.transpose(lft_all, (0, 1, 3, 2)), ckv_all,
            jnp.transpose(krt_all, (0, 1, 3, 2)),
            mk_all, mv_all,
            st(rows_s, 0).reshape(depth, bs, ss, FOX_HEADS, FOX_HEAD_DIM), st(rows_s, 1).reshape(depth, bs, ss, FOX_HEADS, FOX_HEAD_DIM),
            st(rows_s, 2)[..., lo:lo + FOX_HEADS], st(rows_s, 3), st(rows_s, 4))
```
